```python
import math
import jax, jax.numpy as jnp
from jax import lax
import numpy as np

D_MODEL = 1024
BATCH = 16
SEQ = 4096
DEPTH = 2
DEC_BATCH = 2
DEC_SEQ = 8192
PAST_LEN = 128

GRID_W = 64
N_Q_HEADS = 16
N_KV_HEADS = 4
HEAD_DIM = 64
Q_BLOCK = 128
ROPE_THETA = 10000.0
AXIS_DIM = HEAD_DIM // 2
Q_WIDTH = N_Q_HEADS * HEAD_DIM
KV_WIDTH = N_KV_HEADS * HEAD_DIM
SSM_WIDTH = 512
SSM_GROUP = 16
N_SSM_GROUPS = SSM_WIDTH // SSM_GROUP
SSM_STATE = 64
N_DIRECTIONS = 2
STEP_MIN = 1e-3
STEP_MAX = 1e-1
D_FF = 4 * D_MODEL
EPS = 1e-6
IN_WIDTH = Q_WIDTH + 2 * KV_WIDTH + SSM_WIDTH + 2 * D_MODEL
SPLIT_POINTS = (
    Q_WIDTH,
    Q_WIDTH + KV_WIDTH,
    Q_WIDTH + 2 * KV_WIDTH,
    Q_WIDTH + 2 * KV_WIDTH + SSM_WIDTH,
    Q_WIDTH + 2 * KV_WIDTH + SSM_WIDTH + D_MODEL,
)

kernel_name = "hybrid_gated_s5_axial_gqa_encoder"


def rms_norm(x, gain):
    xf = x.astype(jnp.float32)
    xf = xf * lax.rsqrt(jnp.mean(xf * xf, axis=-1, keepdims=True) + EPS)
    return (xf * gain.astype(jnp.float32)).astype(x.dtype)


def axial_rope_tables(seq_len):
    rows = seq_len // GRID_W
    row_idx = jnp.broadcast_to(jnp.arange(rows)[:, None], (rows, GRID_W)).reshape(seq_len)
    col_idx = jnp.broadcast_to(jnp.arange(GRID_W)[None, :], (rows, GRID_W)).reshape(seq_len)
    inv_freq = ROPE_THETA ** (-jnp.arange(0, AXIS_DIM, 2, dtype=jnp.float32) / AXIS_DIM)
    ang_r = row_idx.astype(jnp.float32)[:, None] * inv_freq[None, :]
    ang_c = col_idx.astype(jnp.float32)[:, None] * inv_freq[None, :]
    ang = jnp.concatenate([ang_r, ang_c], axis=-1)
    return jnp.cos(ang), jnp.sin(ang)


def apply_axial_rope(x, cos, sin):
    xf = x.astype(jnp.float32).reshape(*x.shape[:-1], HEAD_DIM // 2, 2)
    x1, x2 = xf[..., 0], xf[..., 1]
    c = cos[:, None, :]
    s = sin[:, None, :]
    out = jnp.stack([x1 * c - x2 * s, x1 * s + x2 * c], axis=-1).reshape(x.shape)
    return out.astype(x.dtype)


def gqa_attention(q, k, v):
    b, l = q.shape[0], q.shape[1]
    rep = N_Q_HEADS // N_KV_HEADS
    n_blocks = l // Q_BLOCK
    qb = q.reshape(b, n_blocks, Q_BLOCK, N_KV_HEADS, rep, HEAD_DIM).transpose(1, 0, 2, 3, 4, 5)
    scale = HEAD_DIM ** -0.5

    def block(q_blk):
        s = jnp.einsum("bqgrd,bkgd->bgrqk", q_blk, k,
                       preferred_element_type=jnp.float32) * scale
        p = jax.nn.softmax(s, axis=-1)
        return jnp.einsum("bgrqk,bkgd->bqgrd", p.astype(v.dtype), v)

    o = lax.map(block, qb)
    return o.transpose(1, 0, 2, 3, 4, 5).reshape(b, l, Q_WIDTH)


def _linear_recurrence(e1, e2):
    a1, x1 = e1
    a2, x2 = e2
    return a1 * a2, a2 * x1 + x2


def s5_bidirectional(u, lam_re, lam_im, log_step, b_re, b_im, c_re, c_im, d_skip):
    bsz, l, _ = u.shape
    uf = u.astype(jnp.float32).reshape(bsz, l, N_SSM_GROUPS, SSM_GROUP)
    lam = lax.complex(lam_re.astype(jnp.float32), lam_im.astype(jnp.float32))
    step = jnp.exp(log_step.astype(jnp.float32))
    lam_bar = jnp.exp(lam * step[..., None])
    b = lax.complex(b_re.astype(jnp.float32), b_im.astype(jnp.float32))
    b_bar = ((lam_bar - 1.0) / lam)[..., None] * b
    c = lax.complex(c_re.astype(jnp.float32), c_im.astype(jnp.float32))

    def one_sequence(useq):
        ucx = useq.astype(jnp.complex64)
        outs = []
        for direction, rev in ((0, False), (1, True)):
            bu = jnp.einsum("gpc,lgc->lgp", b_bar[direction], ucx)
            a = jnp.broadcast_to(lam_bar[direction], bu.shape)
            _, h = lax.associative_scan(_linear_recurrence, (a, bu), reverse=rev, axis=0)
            outs.append(jnp.einsum("gcp,lgp->lgc", c[direction], h).real)
        return outs[0] + outs[1]

    y = lax.map(one_sequence, uf)
    y = y + d_skip.astype(jnp.float32).reshape(N_SSM_GROUPS, SSM_GROUP) * uf
    return y.reshape(bsz, l, SSM_WIDTH).astype(u.dtype)


def encoder_layer(x, cos, sin, norm_mix, w_in, q_norm, k_norm, w_attn_proj,
                  lam_re, lam_im, log_step, b_re, b_im, c_re, c_im, d_skip,
                  w_glu_a, w_glu_b, w_out, norm_mlp, w_ff1, w_ff2):
    bsz, l, _ = x.shape
    xn = rms_norm(x, norm_mix)
    proj = xn @ w_in
    q, k, v, u, g_attn, g_ssm = jnp.split(proj, SPLIT_POINTS, axis=-1)

    q = rms_norm(q.reshape(bsz, l, N_Q_HEADS, HEAD_DIM), q_norm)
    k = rms_norm(k.reshape(bsz, l, N_KV_HEADS, HEAD_DIM), k_norm)
    q = apply_axial_rope(q, cos, sin)
    k = apply_axial_rope(k, cos, sin)
    v = v.reshape(bsz, l, N_KV_HEADS, HEAD_DIM)
    attn_out = gqa_attention(q, k, v) @ w_attn_proj

    y = jax.nn.gelu(s5_bidirectional(u, lam_re, lam_im, log_step, b_re, b_im, c_re, c_im, d_skip))
    ssm_out = (y @ w_glu_a) * jax.nn.sigmoid(y @ w_glu_b)

    merged = jax.nn.sigmoid(g_attn) * attn_out + jax.nn.sigmoid(g_ssm) * ssm_out
    x = x + merged @ w_out

    h = rms_norm(x, norm_mlp)
    x = x + jnp.square(jax.nn.relu(h @ w_ff1)) @ w_ff2
    return x


def setup_inputs(seed: int = 0) -> dict:
    key = jax.random.key(seed)
    ks = jax.random.split(key, 24)
    f32 = jnp.float32

    def normal(k, shape, scale):
        return jax.random.normal(k, shape, f32) * scale

    n_idx = jnp.arange(SSM_STATE, dtype=f32)
    lam_shape = (DEPTH, N_DIRECTIONS, N_SSM_GROUPS, SSM_STATE)
    return {
        "x_prompt": jax.random.normal(ks[0], (BATCH, SEQ, D_MODEL), f32),
        "x_sample": jax.random.normal(ks[1], (DEC_BATCH, DEC_SEQ, D_MODEL), f32),
        "norm_mix": 1.0 + normal(ks[2], (DEPTH, D_MODEL), 0.02),
        "w_in": normal(ks[3], (DEPTH, D_MODEL, IN_WIDTH), D_MODEL ** -0.5),
        "q_norm": 1.0 + normal(ks[4], (DEPTH, HEAD_DIM), 0.02),
        "k_norm": 1.0 + normal(ks[5], (DEPTH, HEAD_DIM), 0.02),
        "w_attn_proj": normal(ks[6], (DEPTH, Q_WIDTH, D_MODEL), Q_WIDTH ** -0.5),
        "ssm_lambda_re": -0.5 + normal(ks[7], lam_shape, 0.01),
        "ssm_lambda_im": math.pi * n_idx + normal(ks[8], lam_shape, 0.01),
        "ssm_log_step": jax.random.uniform(ks[9], (DEPTH, N_DIRECTIONS, N_SSM_GROUPS), f32,
                                           math.log(STEP_MIN), math.log(STEP_MAX)),
        "ssm_b_re": normal(ks[10], (DEPTH, N_DIRECTIONS, N_SSM_GROUPS, SSM_STATE, SSM_GROUP), SSM_GROUP ** -0.5),
        "ssm_b_im": normal(ks[11], (DEPTH, N_DIRECTIONS, N_SSM_GROUPS, SSM_STATE, SSM_GROUP), SSM_GROUP ** -0.5),
        "ssm_c_re": normal(ks[12], (DEPTH, N_DIRECTIONS, N_SSM_GROUPS, SSM_GROUP, SSM_STATE), SSM_STATE ** -0.5),
        "ssm_c_im": normal(ks[13], (DEPTH, N_DIRECTIONS, N_SSM_GROUPS, SSM_GROUP, SSM_STATE), SSM_STATE ** -0.5),
        "ssm_d": normal(ks[14], (DEPTH, SSM_WIDTH), 1.0),
        "w_glu_a": normal(ks[15], (DEPTH, SSM_WIDTH, D_MODEL), SSM_WIDTH ** -0.5),
        "w_glu_b": normal(ks[16], (DEPTH, SSM_WIDTH, D_MODEL), SSM_WIDTH ** -0.5),
        "w_out": normal(ks[17], (DEPTH, D_MODEL, D_MODEL), D_MODEL ** -0.5),
        "norm_mlp": 1.0 + normal(ks[18], (DEPTH, D_MODEL), 0.02),
        "w_ff1": normal(ks[19], (DEPTH, D_MODEL, D_FF), D_MODEL ** -0.5),
        "w_ff2": normal(ks[20], (DEPTH, D_FF, D_MODEL), D_FF ** -0.5),
    }


def reference(x_prompt, x_sample, norm_mix, w_in, q_norm, k_norm, w_attn_proj,
              ssm_lambda_re, ssm_lambda_im, ssm_log_step, ssm_b_re, ssm_b_im,
              ssm_c_re, ssm_c_im, ssm_d, w_glu_a, w_glu_b, w_out, norm_mlp, w_ff1, w_ff2):
    def trunk(x):
        cos, sin = axial_rope_tables(x.shape[1])
        for i in range(DEPTH):
            x = encoder_layer(
                x, cos, sin, norm_mix[i], w_in[i], q_norm[i], k_norm[i], w_attn_proj[i],
                ssm_lambda_re[i], ssm_lambda_im[i], ssm_log_step[i], ssm_b_re[i], ssm_b_im[i],
                ssm_c_re[i], ssm_c_im[i], ssm_d[i], w_glu_a[i], w_glu_b[i], w_out[i],
                norm_mlp[i], w_ff1[i], w_ff2[i])
        return x

    y_prompt = trunk(x_prompt)
    y_sample = trunk(x_sample)
    return (y_prompt, y_sample)
```

```python
import functools
import math

import jax
import jax.numpy as jnp
import numpy as np
from jax import lax
from jax.experimental import pallas as pl
from jax.experimental.pallas import tpu as pltpu

D_MODEL = 1024
GRID_W = 64
N_Q_HEADS = 16
N_KV_HEADS = 4
HEAD_DIM = 64
ROPE_THETA = 10000.0
AXIS_DIM = HEAD_DIM // 2
Q_WIDTH = N_Q_HEADS * HEAD_DIM
KV_WIDTH = N_KV_HEADS * HEAD_DIM
SSM_WIDTH = 512
SSM_GROUP = 16
N_SSM_GROUPS = SSM_WIDTH // SSM_GROUP
SSM_STATE = 64
D_FF = 4 * D_MODEL
EPS = 1e-6

LANES = 128
SUBLANES = 8
CHUNK = 16
CHUNK_COLS = CHUNK * SSM_GROUP
STATE_COLS = 4 * LANES
KV_PAIRS = N_KV_HEADS // 2
Q_PER_KV = N_Q_HEADS // N_KV_HEADS
VMEM_LIMIT = 56 * 1024 * 1024

BF16 = jnp.bfloat16
F32 = jnp.float32


def _cparams(*sem):
    return pltpu.CompilerParams(dimension_semantics=sem, vmem_limit_bytes=VMEM_LIMIT)


def _in_proj_kernel(x_ref, gain_ref, wq_ref, wk_ref, wvt_ref, wu_ref, wga_ref, wgs_ref,
                    qn_ref, kn_ref, bd_ref, cos_ref, sa_ref, sb_ref,
                    q_ref, k_ref, vt_ref, u_ref, ga_ref, gs_ref):
    x = x_ref[...]
    xn = x * lax.rsqrt(jnp.mean(x * x, axis=-1, keepdims=True) + EPS) * gain_ref[...]
    xn = xn.astype(BF16)
    cos = cos_ref[...]
    sa = sa_ref[...]
    sb = sb_ref[...]
    bd = bd_ref[...]

    def head_norm_rope(blk, gain, scale):
        msq = jnp.dot((blk * blk).astype(BF16), bd, preferred_element_type=F32)
        y = blk * lax.rsqrt(msq + EPS) * gain
        y = y * cos + pltpu.roll(y, LANES - 1, 1) * sa + pltpu.roll(y, 1, 1) * sb
        return y * scale

    q = jnp.dot(xn, wq_ref[...], preferred_element_type=F32)
    for b in range(Q_WIDTH // LANES):
        sl = slice(b * LANES, (b + 1) * LANES)
        q_ref[:, sl] = head_norm_rope(q[:, sl], qn_ref[...], HEAD_DIM ** -0.5).astype(BF16)
    k = jnp.dot(xn, wk_ref[...], preferred_element_type=F32)
    for b in range(KV_WIDTH // LANES):
        sl = slice(b * LANES, (b + 1) * LANES)
        k_ref[:, sl] = head_norm_rope(k[:, sl], kn_ref[...], 1.0).astype(BF16)
    vt = lax.dot_general(wvt_ref[...], xn, (((1,), (1,)), ((), ())), preferred_element_type=F32)
    vt_ref[...] = vt.astype(BF16)
    u_ref[...] = jnp.dot(xn, wu_ref[...], preferred_element_type=F32).astype(BF16)
    ga = jnp.dot(xn, wga_ref[...], preferred_element_type=F32)
    ga_ref[...] = jax.nn.sigmoid(ga).astype(BF16)
    gs = jnp.dot(xn, wgs_ref[...], preferred_element_type=F32)
    gs_ref[...] = jax.nn.sigmoid(gs).astype(BF16)


def _in_proj(x2, lw, rope, seq_len, tm):
    n = x2.shape[0]
    tiles_per_seq = seq_len // tm
    row = lambda i: (i, 0)
    const = lambda i: (0, 0)
    pos = lambda i: (i % tiles_per_seq, 0)
    full = lambda a: pl.BlockSpec(a.shape, const)
    cos, sa, sb = rope
    in_specs = [
        pl.BlockSpec((tm, D_MODEL), row), full(lw["norm_mix"]),
        full(lw["wq"]), full(lw["wk"]), full(lw["wvt"]), full(lw["wu"]), full(lw["wga"]),
        full(lw["wgs"]), full(lw["qn"]), full(lw["kn"]), full(lw["bd"]),
        pl.BlockSpec((tm, LANES), pos), pl.BlockSpec((tm, LANES), pos),
        pl.BlockSpec((tm, LANES), pos),
    ]
    out_shape = [
        jax.ShapeDtypeStruct((n, Q_WIDTH), BF16), jax.ShapeDtypeStruct((n, KV_WIDTH), BF16),
        jax.ShapeDtypeStruct((KV_WIDTH, n), BF16), jax.ShapeDtypeStruct((n, SSM_WIDTH), BF16),
        jax.ShapeDtypeStruct((n, D_MODEL), BF16), jax.ShapeDtypeStruct((n, D_MODEL), BF16),
    ]
    out_specs = [
        pl.BlockSpec((tm, Q_WIDTH), row), pl.BlockSpec((tm, KV_WIDTH), row),
        pl.BlockSpec((KV_WIDTH, tm), lambda i: (0, i)), pl.BlockSpec((tm, SSM_WIDTH), row),
        pl.BlockSpec((tm, D_MODEL), row), pl.BlockSpec((tm, D_MODEL), row),
    ]
    return pl.pallas_call(
        _in_proj_kernel, grid=(n // tm,), in_specs=in_specs, out_specs=out_specs,
        out_shape=out_shape, compiler_params=_cparams("parallel"), name="in_proj",
    )(x2, lw["norm_mix"], lw["wq"], lw["wk"], lw["wvt"], lw["wu"], lw["wga"], lw["wgs"],
      lw["qn"], lw["kn"], lw["bd"], cos, sa, sb)


def _attention_kernel(q_ref, k_ref, vt_ref, o_ref, *, tk):
    tq = q_ref.shape[0]
    n_kt = k_ref.shape[0] // tk
    lane = lax.broadcasted_iota(jnp.int32, (tq, LANES), 1)
    row = lax.broadcasted_iota(jnp.int32, (LANES, tq), 0)
    nt = (((1,), (1,)), ((), ()))
    for m in range(Q_PER_KV):
        qm = q_ref[:, m * LANES:(m + 1) * LANES]
        q_a = jnp.where(lane < HEAD_DIM, qm, jnp.zeros_like(qm))
        q_b = jnp.where(lane >= HEAD_DIM, qm, jnp.zeros_like(qm))

        def body(t, carry):
            start = pl.multiple_of(t * tk, tk)
            kt = k_ref[pl.ds(start, tk), :]
            vt = vt_ref[:, pl.ds(start, tk)]
            new = []
            for qh, (mx, l, acc) in zip((q_a, q_b), carry):
                s = lax.dot_general(kt, qh, nt, preferred_element_type=F32)
                mx_new = jnp.maximum(mx, jnp.max(s, axis=0, keepdims=True))
                alpha = jnp.exp(mx - mx_new)
                p = jnp.exp(s - mx_new)
                l_new = alpha * l + jnp.sum(p, axis=0, keepdims=True)
                acc_new = alpha * acc + jnp.dot(vt, p.astype(BF16), preferred_element_type=F32)
                new.append((mx_new, l_new, acc_new))
            return tuple(new)

        init = tuple((jnp.full((1, tq), -jnp.inf, F32), jnp.zeros((1, tq), F32),
                      jnp.zeros((LANES, tq), F32)) for _ in range(2))
        (_, l_a, acc_a), (_, l_b, acc_b) = lax.fori_loop(0, n_kt, body, init)
        o_t = jnp.where(row < HEAD_DIM, acc_a / l_a, acc_b / l_b)
        o_ref[:, m * LANES:(m + 1) * LANES] = o_t.T.astype(BF16)


def _attention(q, k, vt, batch, seq_len, tq, tk):
    n = q.shape[0]
    q_tiles = seq_len // tq
    pair_w = Q_PER_KV * LANES
    qmap = lambda b, j, i: (b * q_tiles + i, j)
    return pl.pallas_call(
        functools.partial(_attention_kernel, tk=tk),
        grid=(batch, KV_PAIRS, q_tiles),
        in_specs=[pl.BlockSpec((tq, pair_w), qmap),
                  pl.BlockSpec((seq_len, LANES), lambda b, j, i: (b, j)),
                  pl.BlockSpec((LANES, seq_len), lambda b, j, i: (j, b))],
        out_specs=pl.BlockSpec((tq, pair_w), qmap),
        out_shape=jax.ShapeDtypeStruct((n, Q_WIDTH), BF16),
        compiler_params=_cparams("parallel", "parallel", "parallel"), name="attention",
    )(q, k, vt)


def _cmul(ar, ai, xr, xi):
    return ar * xr - ai * xi, ar * xi + ai * xr


def _s5_kernel(u_ref, a_ref, bin_ref, cout_ref, dec_ref, y_ref, s_ref):
    n_tiles = u_ref.shape[0] // SUBLANES
    u = u_ref[...]
    s_ref[...] = jnp.dot(u, bin_ref[0], preferred_element_type=F32)
    f_re, f_im = slice(0, LANES), slice(LANES, 2 * LANES)
    b_re, b_im = slice(2 * LANES, 3 * LANES), slice(3 * LANES, 4 * LANES)
    rows = lax.broadcasted_iota(jnp.int32, (SUBLANES, LANES), 0)

    def scan_tile(xr, xi, cr, ci, re, im, down):
        for k in range(3):
            shift = (1 << k) if down else SUBLANES - (1 << k)
            pr, pi = _cmul(dec_ref[0, k, :, re], dec_ref[0, k, :, im],
                           pltpu.roll(xr, shift, 0), pltpu.roll(xi, shift, 0))
            xr, xi = xr + pr, xi + pi
        one = 1 if down else SUBLANES - 1
        edge = 0 if down else SUBLANES - 1
        er, ei = _cmul(dec_ref[0, 3, :, re], dec_ref[0, 3, :, im], cr, ci)
        er = er + jnp.where(rows == edge, 0.0, pltpu.roll(xr, one, 0))
        ei = ei + jnp.where(rows == edge, 0.0, pltpu.roll(xi, one, 0))
        last = SUBLANES - 1 - edge
        nr, ni = _cmul(dec_ref[0, 4, 0:1, re], dec_ref[0, 4, 0:1, im], cr, ci)
        return er, ei, nr + xr[last:last + 1, :], ni + xi[last:last + 1, :]

    def step(t, carry):
        hr, hi, gr, gi = carry
        rf = pl.ds(pl.multiple_of(t * SUBLANES, SUBLANES), SUBLANES)
        rb = pl.ds(pl.multiple_of((n_tiles - 1 - t) * SUBLANES, SUBLANES), SUBLANES)
        er, ei, hr, hi = scan_tile(s_ref[rf, f_re], s_ref[rf, f_im], hr, hi, f_re, f_im, True)
        s_ref[rf, f_re] = er
        s_ref[rf, f_im] = ei
        er, ei, gr, gi = scan_tile(s_ref[rb, b_re], s_ref[rb, b_im], gr, gi, b_re, b_im, False)
        s_ref[rb, b_re] = er
        s_ref[rb, b_im] = ei
        return hr, hi, gr, gi

    zero = jnp.zeros((1, LANES), F32)
    lax.fori_loop(0, n_tiles, step, (zero, zero, zero, zero), unroll=2)
    y = jnp.dot(u, a_ref[0], preferred_element_type=F32)
    y = y + jnp.dot(s_ref[...].astype(BF16), cout_ref[0], preferred_element_type=F32)
    y_ref[...] = y


def _s5(ut, lw, batch, n_chunks):
    g3 = lambda s, g: (g, 0, 0)
    return pl.pallas_call(
        _s5_kernel, grid=(batch, N_SSM_GROUPS),
        in_specs=[pl.BlockSpec((n_chunks, CHUNK_COLS), lambda s, g: (s, g)),
                  pl.BlockSpec((1, CHUNK_COLS, CHUNK_COLS), g3),
                  pl.BlockSpec((1, CHUNK_COLS, STATE_COLS), g3),
                  pl.BlockSpec((1, STATE_COLS, CHUNK_COLS), g3),
                  pl.BlockSpec((1, 5, SUBLANES, STATE_COLS), lambda s, g: (g, 0, 0, 0))],
        out_specs=pl.BlockSpec((n_chunks, CHUNK_COLS), lambda s, g: (s, g)),
        out_shape=jax.ShapeDtypeStruct(ut.shape, F32),
        scratch_shapes=[pltpu.VMEM((n_chunks, STATE_COLS), F32)],
        compiler_params=_cparams("parallel", "parallel"), name="s5",
    )(ut, lw["s5_a"], lw["s5_bin"], lw["s5_cout"], lw["s5_dec"])


def _s5_operators(lam_re, lam_im, log_step, b_re, b_im, c_re, c_im, d_skip):
    t = CHUNK
    lam = lax.complex(lam_re.astype(F32), lam_im.astype(F32))
    step = jnp.exp(log_step.astype(F32))
    lam_dt = lam * step[..., None]
    lam_bar = jnp.exp(lam_dt)
    b_bar = ((lam_bar - 1.0) / lam)[..., None] * lax.complex(b_re.astype(F32), b_im.astype(F32))
    c = lax.complex(c_re.astype(F32), c_im.astype(F32))
    powers = jnp.exp(lam_dt[None] * jnp.arange(t + 1, dtype=F32)[:, None, None, None])
    taps = jnp.einsum("dgop,ndgp,dgpi->ndgoi", c, powers[:t], b_bar).real
    idx = jnp.arange(t)
    lag = idx[None, :] - idx[:, None]
    fwd = jnp.where((lag >= 0)[:, :, None, None, None], taps[jnp.clip(lag, 0, t - 1), 0], 0.0)
    bwd = jnp.where((lag <= 0)[:, :, None, None, None], taps[jnp.clip(-lag, 0, t - 1), 1], 0.0)
    k_ji = fwd + bwd
    eye_t = jnp.eye(t, dtype=F32)
    eye_c = jnp.eye(SSM_GROUP, dtype=F32)
    skip = (eye_t[:, :, None, None, None] * eye_c[None, None, None]
            * d_skip.astype(F32).reshape(N_SSM_GROUPS, SSM_GROUP)[None, None, :, :, None])
    a_op = (k_ji + skip).transpose(2, 0, 4, 1, 3).reshape(N_SSM_GROUPS, CHUNK_COLS, CHUNK_COLS)

    pad = jnp.zeros((N_SSM_GROUPS, CHUNK_COLS, LANES - SSM_STATE), F32)
    in_f = jnp.einsum("jgp,gpi->gjip", powers[t - 1 - idx, 0], b_bar[0])
    in_b = jnp.einsum("jgp,gpi->gjip", powers[idx, 1], b_bar[1])
    cols = []
    for z in (in_f, in_b):
        z = z.reshape(N_SSM_GROUPS, CHUNK_COLS, SSM_STATE)
        cols += [z.real, pad, z.imag, pad]
    bin_op = jnp.concatenate(cols, axis=-1)
    out_f = jnp.einsum("gop,igp->gpio", c[0], powers[idx + 1, 0])
    out_b = jnp.einsum("gop,igp->gpio", c[1], powers[t - idx, 1])
    padr = jnp.zeros((N_SSM_GROUPS, LANES - SSM_STATE, CHUNK_COLS), F32)
    rows = []
    for z in (out_f, out_b):
        z = z.reshape(N_SSM_GROUPS, SSM_STATE, CHUNK_COLS)
        rows += [z.real, padr, -z.imag, padr]
    cout_op = jnp.concatenate(rows, axis=1)
    r = jnp.arange(SUBLANES)
    tile_pow = jnp.exp(lam_dt[None] * (t * jnp.arange(SUBLANES + 1, dtype=F32))[:, None, None, None])

    def scan_consts(d, down):
        z = []
        for k in (1, 2, 4):
            keep = (r >= k) if down else (r <= SUBLANES - 1 - k)
            z.append(jnp.where(keep[:, None, None], tile_pow[k, d][None], 0.0))
        z.append(tile_pow[r if down else SUBLANES - 1 - r, d])
        z.append(jnp.broadcast_to(tile_pow[SUBLANES, d][None], z[0].shape))
        return jnp.stack(z)

    dpad = jnp.zeros((5, SUBLANES, N_SSM_GROUPS, LANES - SSM_STATE), F32)
    zf, zb = scan_consts(0, True), scan_consts(1, False)
    dec = jnp.concatenate([zf.real, dpad, zf.imag, dpad, zb.real, dpad, zb.imag, dpad], axis=-1)
    dec = dec.transpose(2, 0, 1, 3)
    return a_op.astype(BF16), bin_op.astype(BF16), cout_op.astype(BF16), dec


def _mix_kernel(x_ref, attn_ref, y_ref, ga_ref, gs_ref, wp_ref, wa_ref, wb_ref, wo_ref, o_ref):
    attn_out = jnp.dot(attn_ref[...], wp_ref[...], preferred_element_type=F32)
    yg = jax.nn.gelu(y_ref[...]).astype(BF16)
    ssm_out = (jnp.dot(yg, wa_ref[...], preferred_element_type=F32)
               * jax.nn.sigmoid(jnp.dot(yg, wb_ref[...], preferred_element_type=F32)))
    merged = ga_ref[...].astype(F32) * attn_out + gs_ref[...].astype(F32) * ssm_out
    o_ref[...] = x_ref[...] + jnp.dot(merged.astype(BF16), wo_ref[...],
                                      preferred_element_type=F32)


def _mix(x2, attn, y, ga, gs, lw, tm):
    n = x2.shape[0]
    row = lambda i: (i, 0)
    full = lambda a: pl.BlockSpec(a.shape, lambda i: (0, 0))
    return pl.pallas_call(
        _mix_kernel, grid=(n // tm,),
        in_specs=[pl.BlockSpec((tm, D_MODEL), row), pl.BlockSpec((tm, Q_WIDTH), row),
                  pl.BlockSpec((tm, SSM_WIDTH), row), pl.BlockSpec((tm, D_MODEL), row),
                  pl.BlockSpec((tm, D_MODEL), row), full(lw["wp"]), full(lw["wa"]),
                  full(lw["wb"]), full(lw["wo"])],
        out_specs=pl.BlockSpec((tm, D_MODEL), row),
        out_shape=jax.ShapeDtypeStruct((n, D_MODEL), F32),
        compiler_params=_cparams("parallel"), name="mix",
    )(x2, attn, y, ga, gs, lw["wp"], lw["wa"], lw["wb"], lw["wo"])


def _mlp_kernel(x_ref, gain_ref, w1_ref, w2_ref, o_ref):
    x = x_ref[...]
    h = x * lax.rsqrt(jnp.mean(x * x, axis=-1, keepdims=True) + EPS) * gain_ref[...]
    a = jnp.dot(h.astype(BF16), w1_ref[...], preferred_element_type=F32)
    a = jnp.square(jnp.maximum(a, 0.0)).astype(BF16)
    o_ref[...] = x + jnp.dot(a, w2_ref[...], preferred_element_type=F32)


def _mlp(x2, lw, tm):
    n = x2.shape[0]
    row = lambda i: (i, 0)
    full = lambda a: pl.BlockSpec(a.shape, lambda i: (0, 0))
    return pl.pallas_call(
        _mlp_kernel, grid=(n // tm,),
        in_specs=[pl.BlockSpec((tm, D_MODEL), row), full(lw["norm_mlp"]), full(lw["w1"]),
                  full(lw["w2"])],
        out_specs=pl.BlockSpec((tm, D_MODEL), row),
        out_shape=jax.ShapeDtypeStruct((n, D_MODEL), F32),
        compiler_params=_cparams("parallel"), name="mlp",
    )(x2, lw["norm_mlp"], lw["w1"], lw["w2"])


def _q_permutation():
    perm = []
    for j in range(KV_PAIRS):
        for m in range(Q_PER_KV):
            for half in range(2):
                head = Q_PER_KV * (2 * j + half) + m
                perm.extend(range(head * HEAD_DIM, (head + 1) * HEAD_DIM))
    return np.asarray(perm, dtype=np.int32)


def _rope_tables(seq_len):
    t = jnp.arange(seq_len)
    inv_freq = ROPE_THETA ** (-jnp.arange(0, AXIS_DIM, 2, dtype=F32) / AXIS_DIM)
    ang = jnp.concatenate([(t // GRID_W).astype(F32)[:, None] * inv_freq[None, :],
                           (t % GRID_W).astype(F32)[:, None] * inv_freq[None, :]], axis=-1)
    cos = jnp.repeat(jnp.cos(ang), 2, axis=-1)
    sin = jnp.repeat(jnp.sin(ang), 2, axis=-1)
    even = (jnp.arange(HEAD_DIM) % 2 == 0)[None, :]
    sa = jnp.where(even, -sin, 0.0)
    sb = jnp.where(even, 0.0, sin)
    two = lambda a: jnp.concatenate([a, a], axis=-1)
    return two(cos), two(sa), two(sb)


def _layer_weights(i, p):
    perm = _q_permutation()
    w_in = p["w_in"][i]
    o = 0
    wq = w_in[:, o:o + Q_WIDTH]; o += Q_WIDTH
    wk = w_in[:, o:o + KV_WIDTH]; o += KV_WIDTH
    wv = w_in[:, o:o + KV_WIDTH]; o += KV_WIDTH
    wu = w_in[:, o:o + SSM_WIDTH]; o += SSM_WIDTH
    wga = w_in[:, o:o + D_MODEL]; o += D_MODEL
    wgs = w_in[:, o:o + D_MODEL]
    head_avg = jnp.kron(jnp.eye(LANES // HEAD_DIM, dtype=F32),
                        jnp.full((HEAD_DIM, HEAD_DIM), 1.0 / HEAD_DIM, F32))
    two = lambda a: jnp.concatenate([a, a])[None, :].astype(F32)
    a_op, bin_op, cout_op, dec = _s5_operators(
        p["ssm_lambda_re"][i], p["ssm_lambda_im"][i], p["ssm_log_step"][i], p["ssm_b_re"][i],
        p["ssm_b_im"][i], p["ssm_c_re"][i], p["ssm_c_im"][i], p["ssm_d"][i])
    return {
        "norm_mix": p["norm_mix"][i][None, :].astype(F32),
        "wq": wq[:, perm].astype(BF16), "wk": wk.astype(BF16), "wvt": wv.T.astype(BF16),
        "wu": wu.astype(BF16), "wga": wga.astype(BF16), "wgs": wgs.astype(BF16),
        "qn": two(p["q_norm"][i]), "kn": two(p["k_norm"][i]), "bd": head_avg.astype(BF16),
        "s5_a": a_op, "s5_bin": bin_op, "s5_cout": cout_op, "s5_dec": dec,
        "wp": p["w_attn_proj"][i][perm, :].astype(BF16),
        "wa": p["w_glu_a"][i].astype(BF16), "wb": p["w_glu_b"][i].astype(BF16),
        "wo": p["w_out"][i].astype(BF16),
        "norm_mlp": p["norm_mlp"][i][None, :].astype(F32),
        "w1": p["w_ff1"][i].astype(BF16), "w2": p["w_ff2"][i].astype(BF16),
    }


def _tile(n, pref):
    t = min(n, pref)
    assert n % t == 0, (n, t)
    return t


def _trunk(x, layers):
    batch, seq_len, _ = x.shape
    assert seq_len % (CHUNK * 8) == 0 and seq_len % LANES == 0
    n = batch * seq_len
    n_chunks = seq_len // CHUNK
    rope = _rope_tables(seq_len)
    tm = _tile(seq_len, 512)
    tq = _tile(seq_len, 256)
    tk = _tile(seq_len, 512)
    x2 = x.reshape(n, D_MODEL)
    for lw in layers:
        q, k, vt, u, ga, gs = _in_proj(x2, lw, rope, seq_len, tm)
        attn = _attention(q, k, vt, batch, seq_len, tq, tk)
        ut = u.reshape(batch * n_chunks, CHUNK, N_SSM_GROUPS, SSM_GROUP)
        ut = ut.transpose(0, 2, 1, 3).reshape(batch * n_chunks, N_SSM_GROUPS * CHUNK_COLS)
        yt = _s5(ut, lw, batch, n_chunks)
        y = yt.reshape(batch * n_chunks, N_SSM_GROUPS, CHUNK, SSM_GROUP)
        y = y.transpose(0, 2, 1, 3).reshape(n, SSM_WIDTH)
        x2 = _mix(x2, attn, y, ga, gs, lw, tm)
        x2 = _mlp(x2, lw, _tile(seq_len, 256))
    return x2.reshape(batch, seq_len, D_MODEL)


def kernel(x_prompt, x_sample, norm_mix, w_in, q_norm, k_norm, w_attn_proj, ssm_lambda_re, ssm_lambda_im, ssm_log_step, ssm_b_re, ssm_b_im, ssm_c_re, ssm_c_im, ssm_d, w_glu_a, w_glu_b, w_out, norm_mlp, w_ff1, w_ff2):
    params = dict(norm_mix=norm_mix, w_in=w_in, q_norm=q_norm, k_norm=k_norm,
                  w_attn_proj=w_attn_proj, ssm_lambda_re=ssm_lambda_re,
                  ssm_lambda_im=ssm_lambda_im, ssm_log_step=ssm_log_step, ssm_b_re=ssm_b_re,
                  ssm_b_im=ssm_b_im, ssm_c_re=ssm_c_re, ssm_c_im=ssm_c_im, ssm_d=ssm_d,
                  w_glu_a=w_glu_a, w_glu_b=w_glu_b, w_out=w_out, norm_mlp=norm_mlp,
                  w_ff1=w_ff1, w_ff2=w_ff2)
    layers = [_layer_weights(i, params) for i in range(norm_mix.shape[0])]
    return (_trunk(x_prompt, layers), _trunk(x_sample, layers))
```

```python
import functools
import math

import jax
import jax.numpy as jnp
import numpy as np
from jax import lax
from jax.experimental import pallas as pl
from jax.experimental.pallas import tpu as pltpu

D_MODEL = 1024
GRID_W = 64
N_Q_HEADS = 16
N_KV_HEADS = 4
HEAD_DIM = 64
ROPE_THETA = 10000.0
AXIS_DIM = HEAD_DIM // 2
Q_WIDTH = N_Q_HEADS * HEAD_DIM
KV_WIDTH = N_KV_HEADS * HEAD_DIM
SSM_WIDTH = 512
SSM_GROUP = 16
N_SSM_GROUPS = SSM_WIDTH // SSM_GROUP
SSM_STATE = 64
D_FF = 4 * D_MODEL
EPS = 1e-6

LANES = 128
SUBLANES = 8
CHUNK = 16
CHUNK_COLS = CHUNK * SSM_GROUP
STATE_COLS = 4 * LANES
KV_PAIRS = N_KV_HEADS // 2
Q_PER_KV = N_Q_HEADS // N_KV_HEADS
VT_ROWS = KV_PAIRS * 4 * HEAD_DIM
LOG2E = math.log2(math.e)
Q_SCALE = HEAD_DIM ** -0.5 * LOG2E
MAX_UNSHIFTED_SCORE = 100.0
VMEM_LIMIT = 56 * 1024 * 1024

BF16 = jnp.bfloat16
F32 = jnp.float32


def _cparams(*sem):
    return pltpu.CompilerParams(dimension_semantics=sem, vmem_limit_bytes=VMEM_LIMIT)


def _in_proj_kernel(x_ref, gain_ref, wq_ref, wk_ref, wvt_ref, wu_ref, wga_ref, wgs_ref,
                    qn_ref, kn_ref, bd_ref, cos_ref, sa_ref, sb_ref,
                    q_ref, k_ref, vt_ref, u_ref, ga_ref, gs_ref):
    x = x_ref[...]
    xn = x * lax.rsqrt(jnp.mean(x * x, axis=-1, keepdims=True) + EPS) * gain_ref[...]
    xn = xn.astype(BF16)
    cos = cos_ref[...]
    sa = sa_ref[...]
    sb = sb_ref[...]
    bd = bd_ref[...]

    def head_norm_rope(blk, gain, scale):
        msq = jnp.dot((blk * blk).astype(BF16), bd, preferred_element_type=F32)
        y = blk * lax.rsqrt(msq + EPS) * gain
        y = y * cos + pltpu.roll(y, LANES - 1, 1) * sa + pltpu.roll(y, 1, 1) * sb
        return y * scale

    q = jnp.dot(xn, wq_ref[...], preferred_element_type=F32)
    for b in range(Q_WIDTH // LANES):
        sl = slice(b * LANES, (b + 1) * LANES)
        q_ref[:, sl] = head_norm_rope(q[:, sl], qn_ref[...], Q_SCALE).astype(BF16)
    k = jnp.dot(xn, wk_ref[...], preferred_element_type=F32)
    for b in range(KV_WIDTH // LANES):
        sl = slice(b * LANES, (b + 1) * LANES)
        k_ref[:, sl] = head_norm_rope(k[:, sl], kn_ref[...], 1.0).astype(BF16)
    vt = lax.dot_general(wvt_ref[...], xn, (((1,), (1,)), ((), ())), preferred_element_type=F32)
    vt = vt.astype(BF16)
    ones = jnp.ones((2 * HEAD_DIM, vt.shape[1]), BF16)
    for j in range(KV_PAIRS):
        src, dst = 2 * j * HEAD_DIM, 4 * j * HEAD_DIM
        vt_ref[dst:dst + HEAD_DIM, :] = vt[src:src + HEAD_DIM, :]
        vt_ref[dst + HEAD_DIM:dst + 3 * HEAD_DIM, :] = ones
        vt_ref[dst + 3 * HEAD_DIM:dst + 4 * HEAD_DIM, :] = vt[src + HEAD_DIM:src + 2 * HEAD_DIM, :]
    u_ref[...] = jnp.dot(xn, wu_ref[...], preferred_element_type=F32).astype(BF16)
    ga = jnp.dot(xn, wga_ref[...], preferred_element_type=F32)
    ga_ref[...] = jax.nn.sigmoid(ga).astype(BF16)
    gs = jnp.dot(xn, wgs_ref[...], preferred_element_type=F32)
    gs_ref[...] = jax.nn.sigmoid(gs).astype(BF16)


def _in_proj(x2, lw, rope, seq_len, tm):
    n = x2.shape[0]
    tiles_per_seq = seq_len // tm
    row = lambda i: (i, 0)
    const = lambda i: (0, 0)
    pos = lambda i: (i % tiles_per_seq, 0)
    full = lambda a: pl.BlockSpec(a.shape, const)
    cos, sa, sb = rope
    in_specs = [
        pl.BlockSpec((tm, D_MODEL), row), full(lw["norm_mix"]),
        full(lw["wq"]), full(lw["wk"]), full(lw["wvt"]), full(lw["wu"]), full(lw["wga"]),
        full(lw["wgs"]), full(lw["qn"]), full(lw["kn"]), full(lw["bd"]),
        pl.BlockSpec((tm, LANES), pos), pl.BlockSpec((tm, LANES), pos),
        pl.BlockSpec((tm, LANES), pos),
    ]
    out_shape = [
        jax.ShapeDtypeStruct((n, Q_WIDTH), BF16), jax.ShapeDtypeStruct((n, KV_WIDTH), BF16),
        jax.ShapeDtypeStruct((VT_ROWS, n), BF16), jax.ShapeDtypeStruct((n, SSM_WIDTH), BF16),
        jax.ShapeDtypeStruct((n, D_MODEL), BF16), jax.ShapeDtypeStruct((n, D_MODEL), BF16),
    ]
    out_specs = [
        pl.BlockSpec((tm, Q_WIDTH), row), pl.BlockSpec((tm, KV_WIDTH), row),
        pl.BlockSpec((VT_ROWS, tm), lambda i: (0, i)), pl.BlockSpec((tm, SSM_WIDTH), row),
        pl.BlockSpec((tm, D_MODEL), row), pl.BlockSpec((tm, D_MODEL), row),
    ]
    return pl.pallas_call(
        _in_proj_kernel, grid=(n // tm,), in_specs=in_specs, out_specs=out_specs,
        out_shape=out_shape, compiler_params=_cparams("parallel"), name="in_proj",
    )(x2, lw["norm_mix"], lw["wq"], lw["wk"], lw["wvt"], lw["wu"], lw["wga"], lw["wgs"],
      lw["qn"], lw["kn"], lw["bd"], cos, sa, sb)


def _attention_kernel(bounded_ref, q_ref, k_ref, vt_ref, o_ref, *, tk):
    tq = q_ref.shape[0]
    n_kt = k_ref.shape[0] // tk
    lane = lax.broadcasted_iota(jnp.int32, (tq, LANES), 1)
    row = lax.broadcasted_iota(jnp.int32, (LANES, tq), 0)
    nt = (((1,), (1,)), ((), ()))

    def head_pair(m):
        qm = q_ref[:, m * LANES:(m + 1) * LANES]
        return (jnp.where(lane < HEAD_DIM, qm, jnp.zeros_like(qm)),
                jnp.where(lane >= HEAD_DIM, qm, jnp.zeros_like(qm)))

    def tiles(t):
        start = pl.multiple_of(t * tk, tk)
        return (k_ref[pl.ds(start, tk), :], vt_ref[0:LANES, pl.ds(start, tk)],
                vt_ref[LANES:2 * LANES, pl.ds(start, tk)])

    def finish(m, acc_a, acc_b):
        o_t = jnp.where(row < HEAD_DIM, acc_a / acc_a[HEAD_DIM:HEAD_DIM + 1, :],
                        acc_b / acc_b[0:1, :])
        o_ref[:, m * LANES:(m + 1) * LANES] = o_t.T.astype(BF16)

    @pl.when(bounded_ref[0] != 0)
    def _():
        for m in range(Q_PER_KV):
            q_a, q_b = head_pair(m)

            q_ab = jnp.concatenate([q_a, q_b], axis=0)

            def body(t, carry):
                kt, va, vb = tiles(t)
                s = lax.dot_general(kt, q_ab, nt, preferred_element_type=F32)
                p = jnp.exp2(s).astype(BF16)
                return (carry[0] + jnp.dot(va, p[:, :tq], preferred_element_type=F32),
                        carry[1] + jnp.dot(vb, p[:, tq:], preferred_element_type=F32))

            zero = jnp.zeros((LANES, tq), F32)
            acc_a, acc_b = lax.fori_loop(0, n_kt, body, (zero, zero))
            finish(m, acc_a, acc_b)

    @pl.when(bounded_ref[0] == 0)
    def _():
        for m in range(Q_PER_KV):
            q_a, q_b = head_pair(m)

            def body(t, carry):
                kt, va, vb = tiles(t)
                new = []
                for qh, vh, (mx, acc) in zip((q_a, q_b), (va, vb), carry):
                    s = lax.dot_general(kt, qh, nt, preferred_element_type=F32)
                    mx_new = jnp.maximum(mx, jnp.max(s, axis=0, keepdims=True))
                    p = jnp.exp2(s - mx_new).astype(BF16)
                    acc_new = (jnp.exp2(mx - mx_new) * acc
                               + jnp.dot(vh, p, preferred_element_type=F32))
                    new.append((mx_new, acc_new))
                return tuple(new)

            init = tuple((jnp.full((1, tq), -jnp.inf, F32), jnp.zeros((LANES, tq), F32))
                         for _ in range(2))
            (_, acc_a), (_, acc_b) = lax.fori_loop(0, n_kt, body, init)
            finish(m, acc_a, acc_b)


def _attention(bounded, q, k, vt, batch, seq_len, tq, tk):
    n = q.shape[0]
    q_tiles = seq_len // tq
    pair_w = Q_PER_KV * LANES
    qmap = lambda b, j, i: (b * q_tiles + i, j)
    return pl.pallas_call(
        functools.partial(_attention_kernel, tk=tk),
        grid=(batch, KV_PAIRS, q_tiles),
        in_specs=[pl.BlockSpec(memory_space=pltpu.SMEM),
                  pl.BlockSpec((tq, pair_w), qmap),
                  pl.BlockSpec((seq_len, LANES), lambda b, j, i: (b, j)),
                  pl.BlockSpec((2 * LANES, seq_len), lambda b, j, i: (j, b))],
        out_specs=pl.BlockSpec((tq, pair_w), qmap),
        out_shape=jax.ShapeDtypeStruct((n, Q_WIDTH), BF16),
        compiler_params=_cparams("parallel", "parallel", "parallel"), name="attention",
    )(bounded, q, k, vt)


def _cmul(ar, ai, xr, xi):
    return ar * xr - ai * xi, ar * xi + ai * xr


def _s5_kernel(u_ref, a_ref, bin_ref, cout_ref, dec_ref, y_ref, s_ref):
    n_tiles = u_ref.shape[0] // SUBLANES
    u = u_ref[...]
    s_ref[...] = jnp.dot(u, bin_ref[0], preferred_element_type=F32)
    f_re, f_im = slice(0, LANES), slice(LANES, 2 * LANES)
    b_re, b_im = slice(2 * LANES, 3 * LANES), slice(3 * LANES, 4 * LANES)
    rows = lax.broadcasted_iota(jnp.int32, (SUBLANES, LANES), 0)

    def scan_tile(xr, xi, cr, ci, re, im, down):
        for k in range(3):
            shift = (1 << k) if down else SUBLANES - (1 << k)
            pr, pi = _cmul(dec_ref[0, k, :, re], dec_ref[0, k, :, im],
                           pltpu.roll(xr, shift, 0), pltpu.roll(xi, shift, 0))
            xr, xi = xr + pr, xi + pi
        one = 1 if down else SUBLANES - 1
        edge = 0 if down else SUBLANES - 1
        er, ei = _cmul(dec_ref[0, 3, :, re], dec_ref[0, 3, :, im], cr, ci)
        er = er + jnp.where(rows == edge, 0.0, pltpu.roll(xr, one, 0))
        ei = ei + jnp.where(rows == edge, 0.0, pltpu.roll(xi, one, 0))
        last = SUBLANES - 1 - edge
        nr, ni = _cmul(dec_ref[0, 4, 0:1, re], dec_ref[0, 4, 0:1, im], cr, ci)
        return er, ei, nr + xr[last:last + 1, :], ni + xi[last:last + 1, :]

    def step(t, carry):
        hr, hi, gr, gi = carry
        rf = pl.ds(pl.multiple_of(t * SUBLANES, SUBLANES), SUBLANES)
        rb = pl.ds(pl.multiple_of((n_tiles - 1 - t) * SUBLANES, SUBLANES), SUBLANES)
        er, ei, hr, hi = scan_tile(s_ref[rf, f_re], s_ref[rf, f_im], hr, hi, f_re, f_im, True)
        s_ref[rf, f_re] = er
        s_ref[rf, f_im] = ei
        er, ei, gr, gi = scan_tile(s_ref[rb, b_re], s_ref[rb, b_im], gr, gi, b_re, b_im, False)
        s_ref[rb, b_re] = er
        s_ref[rb, b_im] = ei
        return hr, hi, gr, gi

    zero = jnp.zeros((1, LANES), F32)
    lax.fori_loop(0, n_tiles, step, (zero, zero, zero, zero), unroll=2)
    y = jnp.dot(u, a_ref[0], preferred_element_type=F32)
    y = y + jnp.dot(s_ref[...].astype(BF16), cout_ref[0], preferred_element_type=F32)
    y_ref[...] = y


def _s5(ut, lw, batch, n_chunks):
    g3 = lambda s, g: (g, 0, 0)
    return pl.pallas_call(
        _s5_kernel, grid=(batch, N_SSM_GROUPS),
        in_specs=[pl.BlockSpec((n_chunks, CHUNK_COLS), lambda s, g: (s, g)),
                  pl.BlockSpec((1, CHUNK_COLS, CHUNK_COLS), g3),
                  pl.BlockSpec((1, CHUNK_COLS, STATE_COLS), g3),
                  pl.BlockSpec((1, STATE_COLS, CHUNK_COLS), g3),
                  pl.BlockSpec((1, 5, SUBLANES, STATE_COLS), lambda s, g: (g, 0, 0, 0))],
        out_specs=pl.BlockSpec((n_chunks, CHUNK_COLS), lambda s, g: (s, g)),
        out_shape=jax.ShapeDtypeStruct(ut.shape, F32),
        scratch_shapes=[pltpu.VMEM((n_chunks, STATE_COLS), F32)],
        compiler_params=_cparams("parallel", "parallel"), name="s5",
    )(ut, lw["s5_a"], lw["s5_bin"], lw["s5_cout"], lw["s5_dec"])


def _s5_operators(lam_re, lam_im, log_step, b_re, b_im, c_re, c_im, d_skip):
    t = CHUNK
    lam = lax.complex(lam_re.astype(F32), lam_im.astype(F32))
    step = jnp.exp(log_step.astype(F32))
    lam_dt = lam * step[..., None]
    lam_bar = jnp.exp(lam_dt)
    b_bar = ((lam_bar - 1.0) / lam)[..., None] * lax.complex(b_re.astype(F32), b_im.astype(F32))
    c = lax.complex(c_re.astype(F32), c_im.astype(F32))
    powers = jnp.exp(lam_dt[None] * jnp.arange(t + 1, dtype=F32)[:, None, None, None])
    taps = jnp.einsum("dgop,ndgp,dgpi->ndgoi", c, powers[:t], b_bar).real
    idx = jnp.arange(t)
    lag = idx[None, :] - idx[:, None]
    fwd = jnp.where((lag >= 0)[:, :, None, None, None], taps[jnp.clip(lag, 0, t - 1), 0], 0.0)
    bwd = jnp.where((lag <= 0)[:, :, None, None, None], taps[jnp.clip(-lag, 0, t - 1), 1], 0.0)
    k_ji = fwd + bwd
    eye_t = jnp.eye(t, dtype=F32)
    eye_c = jnp.eye(SSM_GROUP, dtype=F32)
    skip = (eye_t[:, :, None, None, None] * eye_c[None, None, None]
            * d_skip.astype(F32).reshape(N_SSM_GROUPS, SSM_GROUP)[None, None, :, :, None])
    a_op = (k_ji + skip).transpose(2, 0, 4, 1, 3).reshape(N_SSM_GROUPS, CHUNK_COLS, CHUNK_COLS)

    pad = jnp.zeros((N_SSM_GROUPS, CHUNK_COLS, LANES - SSM_STATE), F32)
    in_f = jnp.einsum("jgp,gpi->gjip", powers[t - 1 - idx, 0], b_bar[0])
    in_b = jnp.einsum("jgp,gpi->gjip", powers[idx, 1], b_bar[1])
    cols = []
    for z in (in_f, in_b):
        z = z.reshape(N_SSM_GROUPS, CHUNK_COLS, SSM_STATE)
        cols += [z.real, pad, z.imag, pad]
    bin_op = jnp.concatenate(cols, axis=-1)
    out_f = jnp.einsum("gop,igp->gpio", c[0], powers[idx + 1, 0])
    out_b = jnp.einsum("gop,igp->gpio", c[1], powers[t - idx, 1])
    padr = jnp.zeros((N_SSM_GROUPS, LANES - SSM_STATE, CHUNK_COLS), F32)
    rows = []
    for z in (out_f, out_b):
        z = z.reshape(N_SSM_GROUPS, SSM_STATE, CHUNK_COLS)
        rows += [z.real, padr, -z.imag, padr]
    cout_op = jnp.concatenate(rows, axis=1)
    r = jnp.arange(SUBLANES)
    tile_pow = jnp.exp(lam_dt[None] * (t * jnp.arange(SUBLANES + 1, dtype=F32))[:, None, None, None])

    def scan_consts(d, down):
        z = []
        for k in (1, 2, 4):
            keep = (r >= k) if down else (r <= SUBLANES - 1 - k)
            z.append(jnp.where(keep[:, None, None], tile_pow[k, d][None], 0.0))
        z.append(tile_pow[r if down else SUBLANES - 1 - r, d])
        z.append(jnp.broadcast_to(tile_pow[SUBLANES, d][None], z[0].shape))
        return jnp.stack(z)

    dpad = jnp.zeros((5, SUBLANES, N_SSM_GROUPS, LANES - SSM_STATE), F32)
    zf, zb = scan_consts(0, True), scan_consts(1, False)
    dec = jnp.concatenate([zf.real, dpad, zf.imag, dpad, zb.real, dpad, zb.imag, dpad], axis=-1)
    dec = dec.transpose(2, 0, 1, 3)
    return a_op.astype(BF16), bin_op.astype(BF16), cout_op.astype(BF16), dec


def _mix_kernel(x_ref, attn_ref, y_ref, ga_ref, gs_ref, wp_ref, wa_ref, wb_ref, wo_ref, o_ref):
    attn_out = jnp.dot(attn_ref[...], wp_ref[...], preferred_element_type=F32)
    yg = jax.nn.gelu(y_ref[...]).astype(BF16)
    ssm_out = (jnp.dot(yg, wa_ref[...], preferred_element_type=F32)
               * jax.nn.sigmoid(jnp.dot(yg, wb_ref[...], preferred_element_type=F32)))
    merged = ga_ref[...].astype(F32) * attn_out + gs_ref[...].astype(F32) * ssm_out
    o_ref[...] = x_ref[...] + jnp.dot(merged.astype(BF16), wo_ref[...],
                                      preferred_element_type=F32)


def _mix(x2, attn, y, ga, gs, lw, tm):
    n = x2.shape[0]
    row = lambda i: (i, 0)
    full = lambda a: pl.BlockSpec(a.shape, lambda i: (0, 0))
    return pl.pallas_call(
        _mix_kernel, grid=(n // tm,),
        in_specs=[pl.BlockSpec((tm, D_MODEL), row), pl.BlockSpec((tm, Q_WIDTH), row),
                  pl.BlockSpec((tm, SSM_WIDTH), row), pl.BlockSpec((tm, D_MODEL), row),
                  pl.BlockSpec((tm, D_MODEL), row), full(lw["wp"]), full(lw["wa"]),
                  full(lw["wb"]), full(lw["wo"])],
        out_specs=pl.BlockSpec((tm, D_MODEL), row),
        out_shape=jax.ShapeDtypeStruct((n, D_MODEL), F32),
        compiler_params=_cparams("parallel"), name="mix",
    )(x2, attn, y, ga, gs, lw["wp"], lw["wa"], lw["wb"], lw["wo"])


def _mlp_kernel(x_ref, gain_ref, w1_ref, w2_ref, o_ref):
    x = x_ref[...]
    h = x * lax.rsqrt(jnp.mean(x * x, axis=-1, keepdims=True) + EPS) * gain_ref[...]
    a = jnp.dot(h.astype(BF16), w1_ref[...], preferred_element_type=F32)
    a = jnp.square(jnp.maximum(a, 0.0)).astype(BF16)
    o_ref[...] = x + jnp.dot(a, w2_ref[...], preferred_element_type=F32)


def _mlp(x2, lw, tm):
    n = x2.shape[0]
    row = lambda i: (i, 0)
    full = lambda a: pl.BlockSpec(a.shape, lambda i: (0, 0))
    return pl.pallas_call(
        _mlp_kernel, grid=(n // tm,),
        in_specs=[pl.BlockSpec((tm, D_MODEL), row), full(lw["norm_mlp"]), full(lw["w1"]),
                  full(lw["w2"])],
        out_specs=pl.BlockSpec((tm, D_MODEL), row),
        out_shape=jax.ShapeDtypeStruct((n, D_MODEL), F32),
        compiler_params=_cparams("parallel"), name="mlp",
    )(x2, lw["norm_mlp"], lw["w1"], lw["w2"])


def _q_permutation():
    perm = []
    for j in range(KV_PAIRS):
        for m in range(Q_PER_KV):
            for half in range(2):
                head = Q_PER_KV * (2 * j + half) + m
                perm.extend(range(head * HEAD_DIM, (head + 1) * HEAD_DIM))
    return np.asarray(perm, dtype=np.int32)


def _rope_tables(seq_len):
    t = jnp.arange(seq_len)
    inv_freq = ROPE_THETA ** (-jnp.arange(0, AXIS_DIM, 2, dtype=F32) / AXIS_DIM)
    ang = jnp.concatenate([(t // GRID_W).astype(F32)[:, None] * inv_freq[None, :],
                           (t % GRID_W).astype(F32)[:, None] * inv_freq[None, :]], axis=-1)
    cos = jnp.repeat(jnp.cos(ang), 2, axis=-1)
    sin = jnp.repeat(jnp.sin(ang), 2, axis=-1)
    even = (jnp.arange(HEAD_DIM) % 2 == 0)[None, :]
    sa = jnp.where(even, -sin, 0.0)
    sb = jnp.where(even, 0.0, sin)
    two = lambda a: jnp.concatenate([a, a], axis=-1)
    return two(cos), two(sa), two(sb)


def _scores_bounded(q_gain, k_gain):
    bound = (1.05 * HEAD_DIM * Q_SCALE * jnp.max(jnp.abs(q_gain.astype(F32)))
             * jnp.max(jnp.abs(k_gain.astype(F32))))
    return (bound <= MAX_UNSHIFTED_SCORE).astype(jnp.int32).reshape(1)


def _layer_weights(i, p):
    perm = _q_permutation()
    w_in = p["w_in"][i]
    o = 0
    wq = w_in[:, o:o + Q_WIDTH]; o += Q_WIDTH
    wk = w_in[:, o:o + KV_WIDTH]; o += KV_WIDTH
    wv = w_in[:, o:o + KV_WIDTH]; o += KV_WIDTH
    wu = w_in[:, o:o + SSM_WIDTH]; o += SSM_WIDTH
    wga = w_in[:, o:o + D_MODEL]; o += D_MODEL
    wgs = w_in[:, o:o + D_MODEL]
    head_avg = jnp.kron(jnp.eye(LANES // HEAD_DIM, dtype=F32),
                        jnp.full((HEAD_DIM, HEAD_DIM), 1.0 / HEAD_DIM, F32))
    two = lambda a: jnp.concatenate([a, a])[None, :].astype(F32)
    a_op, bin_op, cout_op, dec = _s5_operators(
        p["ssm_lambda_re"][i], p["ssm_lambda_im"][i], p["ssm_log_step"][i], p["ssm_b_re"][i],
        p["ssm_b_im"][i], p["ssm_c_re"][i], p["ssm_c_im"][i], p["ssm_d"][i])
    return {
        "norm_mix": p["norm_mix"][i][None, :].astype(F32),
        "wq": wq[:, perm].astype(BF16), "wk": wk.astype(BF16), "wvt": wv.T.astype(BF16),
        "wu": wu.astype(BF16), "wga": wga.astype(BF16), "wgs": wgs.astype(BF16),
        "qn": two(p["q_norm"][i]), "kn": two(p["k_norm"][i]), "bd": head_avg.astype(BF16),
        "bounded": _scores_bounded(p["q_norm"][i], p["k_norm"][i]),
        "s5_a": a_op, "s5_bin": bin_op, "s5_cout": cout_op, "s5_dec": dec,
        "wp": p["w_attn_proj"][i][perm, :].astype(BF16),
        "wa": p["w_glu_a"][i].astype(BF16), "wb": p["w_glu_b"][i].astype(BF16),
        "wo": p["w_out"][i].astype(BF16),
        "norm_mlp": p["norm_mlp"][i][None, :].astype(F32),
        "w1": p["w_ff1"][i].astype(BF16), "w2": p["w_ff2"][i].astype(BF16),
    }


def _tile(n, pref):
    t = min(n, pref)
    assert n % t == 0, (n, t)
    return t


def _trunk(x, layers):
    batch, seq_len, _ = x.shape
    assert seq_len % (CHUNK * 8) == 0 and seq_len % LANES == 0
    n = batch * seq_len
    n_chunks = seq_len // CHUNK
    rope = _rope_tables(seq_len)
    tm = _tile(seq_len, 512)
    tq = _tile(seq_len, 256)
    tk = _tile(seq_len, 2048)
    x2 = x.reshape(n, D_MODEL)
    for lw in layers:
        q, k, vt, u, ga, gs = _in_proj(x2, lw, rope, seq_len, tm)
        attn = _attention(lw["bounded"], q, k, vt, batch, seq_len, tq, tk)
        ut = u.reshape(batch * n_chunks, CHUNK, N_SSM_GROUPS, SSM_GROUP)
        ut = ut.transpose(0, 2, 1, 3).reshape(batch * n_chunks, N_SSM_GROUPS * CHUNK_COLS)
        yt = _s5(ut, lw, batch, n_chunks)
        y = yt.reshape(batch * n_chunks, N_SSM_GROUPS, CHUNK, SSM_GROUP)
        y = y.transpose(0, 2, 1, 3).reshape(n, SSM_WIDTH)
        x2 = _mix(x2, attn, y, ga, gs, lw, tm)
        x2 = _mlp(x2, lw, _tile(seq_len, 256))
    return x2.reshape(batch, seq_len, D_MODEL)


def kernel(x_prompt, x_sample, norm_mix, w_in, q_norm, k_norm, w_attn_proj, ssm_lambda_re, ssm_lambda_im, ssm_log_step, ssm_b_re, ssm_b_im, ssm_c_re, ssm_c_im, ssm_d, w_glu_a, w_glu_b, w_out, norm_mlp, w_ff1, w_ff2):
    params = dict(norm_mix=norm_mix, w_in=w_in, q_norm=q_norm, k_norm=k_norm,
                  w_attn_proj=w_attn_proj, ssm_lambda_re=ssm_lambda_re,
                  ssm_lambda_im=ssm_lambda_im, ssm_log_step=ssm_log_step, ssm_b_re=ssm_b_re,
                  ssm_b_im=ssm_b_im, ssm_c_re=ssm_c_re, ssm_c_im=ssm_c_im, ssm_d=ssm_d,
                  w_glu_a=w_glu_a, w_glu_b=w_glu_b, w_out=w_out, norm_mlp=norm_mlp,
                  w_ff1=w_ff1, w_ff2=w_ff2)
    layers = [_layer_weights(i, params) for i in range(norm_mix.shape[0])]
    return (_trunk(x_prompt, layers), _trunk(x_sample, layers))
```

```python
import functools
import math

import jax
import jax.numpy as jnp
import numpy as np
from jax import lax
from jax.experimental import pallas as pl
from jax.experimental.pallas import tpu as pltpu

D_MODEL = 1024
GRID_W = 64
N_Q_HEADS = 16
N_KV_HEADS = 4
HEAD_DIM = 64
ROPE_THETA = 10000.0
AXIS_DIM = HEAD_DIM // 2
Q_WIDTH = N_Q_HEADS * HEAD_DIM
KV_WIDTH = N_KV_HEADS * HEAD_DIM
SSM_WIDTH = 512
SSM_GROUP = 16
N_SSM_GROUPS = SSM_WIDTH // SSM_GROUP
SSM_STATE = 64
D_FF = 4 * D_MODEL
EPS = 1e-6

LANES = 128
SUBLANES = 8
CHUNK = 8
GROUPS_PER_BLOCK = LANES // SSM_GROUP
N_CHANNEL_BLOCKS = SSM_WIDTH // LANES
CHUNK_COLS = CHUNK * LANES
STATE_PART = GROUPS_PER_BLOCK * SSM_STATE
STATE_COLS = 4 * STATE_PART
S5_ROW_BLOCK = 256
KV_PAIRS = N_KV_HEADS // 2
Q_PER_KV = N_Q_HEADS // N_KV_HEADS
VT_ROWS = KV_PAIRS * 4 * HEAD_DIM
LOG2E = math.log2(math.e)
Q_SCALE = HEAD_DIM ** -0.5 * LOG2E
MAX_UNSHIFTED_SCORE = 100.0
SHIFTED_KEY_TILE = 512
VMEM_LIMIT = 56 * 1024 * 1024

BF16 = jnp.bfloat16
F32 = jnp.float32


def _cparams(*sem):
    return pltpu.CompilerParams(dimension_semantics=sem, vmem_limit_bytes=VMEM_LIMIT)


def _in_proj_kernel(x_ref, gain_ref, wq_ref, wk_ref, wvt_ref, wu_ref, wga_ref, wgs_ref,
                    qn_ref, kn_ref, bd_ref, cos_ref, sa_ref, sb_ref,
                    q_ref, k_ref, vt_ref, u_ref, ga_ref, gs_ref):
    x = x_ref[...]
    xn = x * lax.rsqrt(jnp.mean(x * x, axis=-1, keepdims=True) + EPS) * gain_ref[...]
    xn = xn.astype(BF16)
    cos = cos_ref[...]
    sa = sa_ref[...]
    sb = sb_ref[...]
    bd = bd_ref[...]

    def head_norm_rope(blk, gain, scale):
        msq = jnp.dot((blk * blk).astype(BF16), bd, preferred_element_type=F32)
        y = blk * lax.rsqrt(msq + EPS) * gain
        y = y * cos + pltpu.roll(y, LANES - 1, 1) * sa + pltpu.roll(y, 1, 1) * sb
        return y * scale

    q = jnp.dot(xn, wq_ref[...], preferred_element_type=F32)
    for b in range(Q_WIDTH // LANES):
        sl = slice(b * LANES, (b + 1) * LANES)
        q_ref[:, sl] = head_norm_rope(q[:, sl], qn_ref[...], Q_SCALE).astype(BF16)
    k = jnp.dot(xn, wk_ref[...], preferred_element_type=F32)
    for b in range(KV_WIDTH // LANES):
        sl = slice(b * LANES, (b + 1) * LANES)
        k_ref[:, sl] = head_norm_rope(k[:, sl], kn_ref[...], 1.0).astype(BF16)
    vt = lax.dot_general(wvt_ref[...], xn, (((1,), (1,)), ((), ())), preferred_element_type=F32)
    vt = vt.astype(BF16)
    ones = jnp.ones((2 * HEAD_DIM, vt.shape[1]), BF16)
    for j in range(KV_PAIRS):
        src, dst = 2 * j * HEAD_DIM, 4 * j * HEAD_DIM
        vt_ref[dst:dst + HEAD_DIM, :] = vt[src:src + HEAD_DIM, :]
        vt_ref[dst + HEAD_DIM:dst + 3 * HEAD_DIM, :] = ones
        vt_ref[dst + 3 * HEAD_DIM:dst + 4 * HEAD_DIM, :] = vt[src + HEAD_DIM:src + 2 * HEAD_DIM, :]
    u_ref[...] = jnp.dot(xn, wu_ref[...], preferred_element_type=F32)
    ga = jnp.dot(xn, wga_ref[...], preferred_element_type=F32)
    ga_ref[...] = jax.nn.sigmoid(ga).astype(BF16)
    gs = jnp.dot(xn, wgs_ref[...], preferred_element_type=F32)
    gs_ref[...] = jax.nn.sigmoid(gs).astype(BF16)


def _in_proj(x2, lw, rope, seq_len, tm):
    n = x2.shape[0]
    tiles_per_seq = seq_len // tm
    row = lambda i: (i, 0)
    const = lambda i: (0, 0)
    pos = lambda i: (i % tiles_per_seq, 0)
    full = lambda a: pl.BlockSpec(a.shape, const)
    cos, sa, sb = rope
    in_specs = [
        pl.BlockSpec((tm, D_MODEL), row), full(lw["norm_mix"]),
        full(lw["wq"]), full(lw["wk"]), full(lw["wvt"]), full(lw["wu"]), full(lw["wga"]),
        full(lw["wgs"]), full(lw["qn"]), full(lw["kn"]), full(lw["bd"]),
        pl.BlockSpec((tm, LANES), pos), pl.BlockSpec((tm, LANES), pos),
        pl.BlockSpec((tm, LANES), pos),
    ]
    out_shape = [
        jax.ShapeDtypeStruct((n, Q_WIDTH), BF16), jax.ShapeDtypeStruct((n, KV_WIDTH), BF16),
        jax.ShapeDtypeStruct((VT_ROWS, n), BF16), jax.ShapeDtypeStruct((n, SSM_WIDTH), F32),
        jax.ShapeDtypeStruct((n, D_MODEL), BF16), jax.ShapeDtypeStruct((n, D_MODEL), BF16),
    ]
    out_specs = [
        pl.BlockSpec((tm, Q_WIDTH), row), pl.BlockSpec((tm, KV_WIDTH), row),
        pl.BlockSpec((VT_ROWS, tm), lambda i: (0, i)), pl.BlockSpec((tm, SSM_WIDTH), row),
        pl.BlockSpec((tm, D_MODEL), row), pl.BlockSpec((tm, D_MODEL), row),
    ]
    return pl.pallas_call(
        _in_proj_kernel, grid=(n // tm,), in_specs=in_specs, out_specs=out_specs,
        out_shape=out_shape, compiler_params=_cparams("parallel"), name="in_proj",
    )(x2, lw["norm_mix"], lw["wq"], lw["wk"], lw["wvt"], lw["wu"], lw["wga"], lw["wgs"],
      lw["qn"], lw["kn"], lw["bd"], cos, sa, sb)


def _attention_kernel(bounded_ref, q_ref, k_ref, vt_ref, o_ref, *, tk):
    tq = q_ref.shape[0]
    seq_len = k_ref.shape[0]
    tk_shifted = min(tk, SHIFTED_KEY_TILE)
    lane = lax.broadcasted_iota(jnp.int32, (tq, LANES), 1)
    row = lax.broadcasted_iota(jnp.int32, (LANES, tq), 0)
    nt = (((1,), (1,)), ((), ()))

    def head_pair(m):
        qm = q_ref[:, m * LANES:(m + 1) * LANES]
        return (jnp.where(lane < HEAD_DIM, qm, jnp.zeros_like(qm)),
                jnp.where(lane >= HEAD_DIM, qm, jnp.zeros_like(qm)))

    def tiles(t, width):
        start = pl.multiple_of(t * width, width)
        return (k_ref[pl.ds(start, width), :], vt_ref[0:LANES, pl.ds(start, width)],
                vt_ref[LANES:2 * LANES, pl.ds(start, width)])

    def finish(m, acc_a, acc_b):
        o_t = jnp.where(row < HEAD_DIM, acc_a / acc_a[HEAD_DIM:HEAD_DIM + 1, :],
                        acc_b / acc_b[0:1, :])
        o_ref[:, m * LANES:(m + 1) * LANES] = o_t.T.astype(BF16)

    @pl.when(bounded_ref[0] != 0)
    def _():
        for m in range(Q_PER_KV):
            q_a, q_b = head_pair(m)

            q_ab = jnp.concatenate([q_a, q_b], axis=0)

            def body(t, carry):
                kt, va, vb = tiles(t, tk)
                s = lax.dot_general(kt, q_ab, nt, preferred_element_type=F32)
                p = jnp.exp2(s).astype(BF16)
                return (carry[0] + jnp.dot(va, p[:, :tq], preferred_element_type=F32),
                        carry[1] + jnp.dot(vb, p[:, tq:], preferred_element_type=F32))

            zero = jnp.zeros((LANES, tq), F32)
            acc_a, acc_b = lax.fori_loop(0, seq_len // tk, body, (zero, zero))
            finish(m, acc_a, acc_b)

    @pl.when(bounded_ref[0] == 0)
    def _():
        for m in range(Q_PER_KV):
            q_a, q_b = head_pair(m)

            def body(t, carry):
                kt, va, vb = tiles(t, tk_shifted)
                new = []
                for qh, vh, (mx, acc) in zip((q_a, q_b), (va, vb), carry):
                    s = lax.dot_general(kt, qh, nt, preferred_element_type=F32)
                    mx_new = jnp.maximum(mx, jnp.max(s, axis=0, keepdims=True))
                    p = jnp.exp2(s - mx_new).astype(BF16)
                    acc_new = (jnp.exp2(mx - mx_new) * acc
                               + jnp.dot(vh, p, preferred_element_type=F32))
                    new.append((mx_new, acc_new))
                return tuple(new)

            init = tuple((jnp.full((1, tq), -jnp.inf, F32), jnp.zeros((LANES, tq), F32))
                         for _ in range(2))
            (_, acc_a), (_, acc_b) = lax.fori_loop(0, seq_len // tk_shifted, body, init)
            finish(m, acc_a, acc_b)


def _attention(bounded, q, k, vt, batch, seq_len, tq, tk):
    n = q.shape[0]
    q_tiles = seq_len // tq
    pair_w = Q_PER_KV * LANES
    qmap = lambda b, j, i: (b * q_tiles + i, j)
    return pl.pallas_call(
        functools.partial(_attention_kernel, tk=tk),
        grid=(batch, KV_PAIRS, q_tiles),
        in_specs=[pl.BlockSpec(memory_space=pltpu.SMEM),
                  pl.BlockSpec((tq, pair_w), qmap),
                  pl.BlockSpec((seq_len, LANES), lambda b, j, i: (b, j)),
                  pl.BlockSpec((2 * LANES, seq_len), lambda b, j, i: (j, b))],
        out_specs=pl.BlockSpec((tq, pair_w), qmap),
        out_shape=jax.ShapeDtypeStruct((n, Q_WIDTH), BF16),
        compiler_params=_cparams("parallel", "parallel", "parallel"), name="attention",
    )(bounded, q, k, vt)


def _cmul(ar, ai, xr, xi):
    return ar * xr - ai * xi, ar * xi + ai * xr


def _s5_kernel(u_ref, a_ref, bin_ref, cout_ref, dec_ref, y_ref, x_ref, s_ref):
    n_chunks = x_ref.shape[0]
    n_tiles = n_chunks // SUBLANES
    row_block = min(S5_ROW_BLOCK, n_chunks)
    for i in range(CHUNK):
        x_ref[:, i * LANES:(i + 1) * LANES] = (
            u_ref[pl.ds(i, n_chunks, stride=CHUNK), :].astype(BF16))
    for r in range(n_chunks // row_block):
        rs = slice(r * row_block, (r + 1) * row_block)
        s_ref[rs, :] = jnp.dot(x_ref[rs, :], bin_ref[0], preferred_element_type=F32)
    f_re, f_im = slice(0, STATE_PART), slice(STATE_PART, 2 * STATE_PART)
    b_re, b_im = slice(2 * STATE_PART, 3 * STATE_PART), slice(3 * STATE_PART, 4 * STATE_PART)
    rows = lax.broadcasted_iota(jnp.int32, (SUBLANES, STATE_PART), 0)

    def scan_tile(xr, xi, cr, ci, re, im, down):
        for k in range(3):
            shift = (1 << k) if down else SUBLANES - (1 << k)
            pr, pi = _cmul(dec_ref[0, k, :, re], dec_ref[0, k, :, im],
                           pltpu.roll(xr, shift, 0), pltpu.roll(xi, shift, 0))
            xr, xi = xr + pr, xi + pi
        one = 1 if down else SUBLANES - 1
        edge = 0 if down else SUBLANES - 1
        er, ei = _cmul(dec_ref[0, 3, :, re], dec_ref[0, 3, :, im], cr, ci)
        er = er + jnp.where(rows == edge, 0.0, pltpu.roll(xr, one, 0))
        ei = ei + jnp.where(rows == edge, 0.0, pltpu.roll(xi, one, 0))
        last = SUBLANES - 1 - edge
        nr, ni = _cmul(dec_ref[0, 4, 0:1, re], dec_ref[0, 4, 0:1, im], cr, ci)
        return er, ei, nr + xr[last:last + 1, :], ni + xi[last:last + 1, :]

    def step(t, carry):
        hr, hi, gr, gi = carry
        rf = pl.ds(pl.multiple_of(t * SUBLANES, SUBLANES), SUBLANES)
        rb = pl.ds(pl.multiple_of((n_tiles - 1 - t) * SUBLANES, SUBLANES), SUBLANES)
        er, ei, hr, hi = scan_tile(s_ref[rf, f_re], s_ref[rf, f_im], hr, hi, f_re, f_im, True)
        s_ref[rf, f_re] = er
        s_ref[rf, f_im] = ei
        er, ei, gr, gi = scan_tile(s_ref[rb, b_re], s_ref[rb, b_im], gr, gi, b_re, b_im, False)
        s_ref[rb, b_re] = er
        s_ref[rb, b_im] = ei
        return hr, hi, gr, gi

    zero = jnp.zeros((1, STATE_PART), F32)
    lax.fori_loop(0, n_tiles, step, (zero, zero, zero, zero))
    for r in range(n_chunks // row_block):
        rs = slice(r * row_block, (r + 1) * row_block)
        y = jnp.dot(x_ref[rs, :], a_ref[0], preferred_element_type=F32)
        y = y + jnp.dot(s_ref[rs, :].astype(BF16), cout_ref[0], preferred_element_type=F32)
        for i in range(CHUNK):
            y_ref[pl.ds(r * row_block * CHUNK + i, row_block, stride=CHUNK), :] = (
                y[:, i * LANES:(i + 1) * LANES])


def _s5(u, lw, batch, seq_len):
    n_chunks = seq_len // CHUNK
    w3 = lambda b, s: (b, 0, 0)
    io = pl.BlockSpec((seq_len, LANES), lambda b, s: (s, b))
    return pl.pallas_call(
        _s5_kernel, grid=(N_CHANNEL_BLOCKS, batch),
        in_specs=[io,
                  pl.BlockSpec((1, CHUNK_COLS, CHUNK_COLS), w3),
                  pl.BlockSpec((1, CHUNK_COLS, STATE_COLS), w3),
                  pl.BlockSpec((1, STATE_COLS, CHUNK_COLS), w3),
                  pl.BlockSpec((1, 5, SUBLANES, STATE_COLS), lambda b, s: (b, 0, 0, 0))],
        out_specs=io,
        out_shape=jax.ShapeDtypeStruct(u.shape, F32),
        scratch_shapes=[pltpu.VMEM((n_chunks, CHUNK_COLS), BF16),
                        pltpu.VMEM((n_chunks, STATE_COLS), F32)],
        compiler_params=_cparams("parallel", "parallel"), name="s5",
    )(u, lw["s5_a"], lw["s5_bin"], lw["s5_cout"], lw["s5_dec"])


def _s5_operators(lam_re, lam_im, log_step, b_re, b_im, c_re, c_im, d_skip):
    t = CHUNK
    lam = lax.complex(lam_re.astype(F32), lam_im.astype(F32))
    step = jnp.exp(log_step.astype(F32))
    lam_dt = lam * step[..., None]
    lam_bar = jnp.exp(lam_dt)
    b_bar = ((lam_bar - 1.0) / lam)[..., None] * lax.complex(b_re.astype(F32), b_im.astype(F32))
    c = lax.complex(c_re.astype(F32), c_im.astype(F32))
    powers = jnp.exp(lam_dt[None] * jnp.arange(t + 1, dtype=F32)[:, None, None, None])
    taps = jnp.einsum("dgop,ndgp,dgpi->ndgoi", c, powers[:t], b_bar).real
    idx = jnp.arange(t)
    lag = idx[None, :] - idx[:, None]
    fwd = jnp.where((lag >= 0)[:, :, None, None, None], taps[jnp.clip(lag, 0, t - 1), 0], 0.0)
    bwd = jnp.where((lag <= 0)[:, :, None, None, None], taps[jnp.clip(-lag, 0, t - 1), 1], 0.0)
    k_ji = fwd + bwd
    eye_t = jnp.eye(t, dtype=F32)
    eye_c = jnp.eye(SSM_GROUP, dtype=F32)
    skip = (eye_t[:, :, None, None, None] * eye_c[None, None, None]
            * d_skip.astype(F32).reshape(N_SSM_GROUPS, SSM_GROUP)[None, None, :, :, None])
    nb, gb = N_CHANNEL_BLOCKS, GROUPS_PER_BLOCK
    same_group = jnp.eye(gb, dtype=F32)
    a_grp = (k_ji + skip).transpose(2, 0, 4, 1, 3).reshape(nb, gb, t, SSM_GROUP, t, SSM_GROUP)
    a_op = jnp.einsum("bgjcio,gh->bjgciho", a_grp, same_group)
    a_op = a_op.reshape(nb, CHUNK_COLS, CHUNK_COLS)
    in_f = jnp.einsum("jgp,gpi->gjip", powers[t - 1 - idx, 0], b_bar[0])
    in_b = jnp.einsum("jgp,gpi->gjip", powers[idx, 1], b_bar[1])
    bin_grp = jnp.stack([in_f.real, in_f.imag, in_b.real, in_b.imag], axis=3)
    bin_grp = bin_grp.reshape(nb, gb, t, SSM_GROUP, 4, SSM_STATE)
    bin_op = jnp.einsum("bgjcqp,gh->bjgcqhp", bin_grp, same_group)
    bin_op = bin_op.reshape(nb, CHUNK_COLS, STATE_COLS)
    out_f = jnp.einsum("gop,igp->gpio", c[0], powers[idx + 1, 0])
    out_b = jnp.einsum("gop,igp->gpio", c[1], powers[t - idx, 1])
    cout_grp = jnp.stack([out_f.real, -out_f.imag, out_b.real, -out_b.imag], axis=1)
    cout_grp = cout_grp.reshape(nb, gb, 4, SSM_STATE, t, SSM_GROUP)
    cout_op = jnp.einsum("bgqpio,gh->bqgpiho", cout_grp, same_group)
    cout_op = cout_op.reshape(nb, STATE_COLS, CHUNK_COLS)
    r = jnp.arange(SUBLANES)
    tile_pow = jnp.exp(lam_dt[None] * (t * jnp.arange(SUBLANES + 1, dtype=F32))[:, None, None, None])

    def scan_consts(d, down):
        z = []
        for k in (1, 2, 4):
            keep = (r >= k) if down else (r <= SUBLANES - 1 - k)
            z.append(jnp.where(keep[:, None, None], tile_pow[k, d][None], 0.0))
        z.append(tile_pow[r if down else SUBLANES - 1 - r, d])
        z.append(jnp.broadcast_to(tile_pow[SUBLANES, d][None], z[0].shape))
        return jnp.stack(z)

    zf, zb = scan_consts(0, True), scan_consts(1, False)
    dec = jnp.stack([zf.real, zf.imag, zb.real, zb.imag], axis=2)
    dec = dec.reshape(5, SUBLANES, 4, nb, gb * SSM_STATE).transpose(3, 0, 1, 2, 4)
    dec = dec.reshape(nb, 5, SUBLANES, STATE_COLS)
    return a_op.astype(BF16), bin_op.astype(BF16), cout_op.astype(BF16), dec


def _mix_kernel(x_ref, attn_ref, y_ref, ga_ref, gs_ref, wp_ref, wa_ref, wb_ref, wo_ref, o_ref):
    attn_out = jnp.dot(attn_ref[...], wp_ref[...], preferred_element_type=F32)
    yg = jax.nn.gelu(y_ref[...]).astype(BF16)
    ssm_out = (jnp.dot(yg, wa_ref[...], preferred_element_type=F32)
               * jax.nn.sigmoid(jnp.dot(yg, wb_ref[...], preferred_element_type=F32)))
    merged = ga_ref[...].astype(F32) * attn_out + gs_ref[...].astype(F32) * ssm_out
    o_ref[...] = x_ref[...] + jnp.dot(merged.astype(BF16), wo_ref[...],
                                      preferred_element_type=F32)


def _mix(x2, attn, y, ga, gs, lw, tm):
    n = x2.shape[0]
    row = lambda i: (i, 0)
    full = lambda a: pl.BlockSpec(a.shape, lambda i: (0, 0))
    return pl.pallas_call(
        _mix_kernel, grid=(n // tm,),
        in_specs=[pl.BlockSpec((tm, D_MODEL), row), pl.BlockSpec((tm, Q_WIDTH), row),
                  pl.BlockSpec((tm, SSM_WIDTH), row), pl.BlockSpec((tm, D_MODEL), row),
                  pl.BlockSpec((tm, D_MODEL), row), full(lw["wp"]), full(lw["wa"]),
                  full(lw["wb"]), full(lw["wo"])],
        out_specs=pl.BlockSpec((tm, D_MODEL), row),
        out_shape=jax.ShapeDtypeStruct((n, D_MODEL), F32),
        compiler_params=_cparams("parallel"), name="mix",
    )(x2, attn, y, ga, gs, lw["wp"], lw["wa"], lw["wb"], lw["wo"])


def _mlp_kernel(x_ref, gain_ref, w1_ref, w2_ref, o_ref):
    x = x_ref[...]
    h = x * lax.rsqrt(jnp.mean(x * x, axis=-1, keepdims=True) + EPS) * gain_ref[...]
    a = jnp.dot(h.astype(BF16), w1_ref[...], preferred_element_type=F32)
    a = jnp.square(jnp.maximum(a, 0.0)).astype(BF16)
    o_ref[...] = x + jnp.dot(a, w2_ref[...], preferred_element_type=F32)


def _mlp(x2, lw, tm):
    n = x2.shape[0]
    row = lambda i: (i, 0)
    full = lambda a: pl.BlockSpec(a.shape, lambda i: (0, 0))
    return pl.pallas_call(
        _mlp_kernel, grid=(n // tm,),
        in_specs=[pl.BlockSpec((tm, D_MODEL), row), full(lw["norm_mlp"]), full(lw["w1"]),
                  full(lw["w2"])],
        out_specs=pl.BlockSpec((tm, D_MODEL), row),
        out_shape=jax.ShapeDtypeStruct((n, D_MODEL), F32),
        compiler_params=_cparams("parallel"), name="mlp",
    )(x2, lw["norm_mlp"], lw["w1"], lw["w2"])


def _q_permutation():
    perm = []
    for j in range(KV_PAIRS):
        for m in range(Q_PER_KV):
            for half in range(2):
                head = Q_PER_KV * (2 * j + half) + m
                perm.extend(range(head * HEAD_DIM, (head + 1) * HEAD_DIM))
    return np.asarray(perm, dtype=np.int32)


def _rope_tables(seq_len):
    t = jnp.arange(seq_len)
    inv_freq = ROPE_THETA ** (-jnp.arange(0, AXIS_DIM, 2, dtype=F32) / AXIS_DIM)
    ang = jnp.concatenate([(t // GRID_W).astype(F32)[:, None] * inv_freq[None, :],
                           (t % GRID_W).astype(F32)[:, None] * inv_freq[None, :]], axis=-1)
    cos = jnp.repeat(jnp.cos(ang), 2, axis=-1)
    sin = jnp.repeat(jnp.sin(ang), 2, axis=-1)
    even = (jnp.arange(HEAD_DIM) % 2 == 0)[None, :]
    sa = jnp.where(even, -sin, 0.0)
    sb = jnp.where(even, 0.0, sin)
    two = lambda a: jnp.concatenate([a, a], axis=-1)
    return two(cos), two(sa), two(sb)


def _scores_bounded(q_gain, k_gain):
    bound = (1.05 * HEAD_DIM * Q_SCALE * jnp.max(jnp.abs(q_gain.astype(F32)))
             * jnp.max(jnp.abs(k_gain.astype(F32))))
    return (bound <= MAX_UNSHIFTED_SCORE).astype(jnp.int32).reshape(1)


def _layer_weights(i, p):
    perm = _q_permutation()
    w_in = p["w_in"][i]
    o = 0
    wq = w_in[:, o:o + Q_WIDTH]; o += Q_WIDTH
    wk = w_in[:, o:o + KV_WIDTH]; o += KV_WIDTH
    wv = w_in[:, o:o + KV_WIDTH]; o += KV_WIDTH
    wu = w_in[:, o:o + SSM_WIDTH]; o += SSM_WIDTH
    wga = w_in[:, o:o + D_MODEL]; o += D_MODEL
    wgs = w_in[:, o:o + D_MODEL]
    head_avg = jnp.kron(jnp.eye(LANES // HEAD_DIM, dtype=F32),
                        jnp.full((HEAD_DIM, HEAD_DIM), 1.0 / HEAD_DIM, F32))
    two = lambda a: jnp.concatenate([a, a])[None, :].astype(F32)
    a_op, bin_op, cout_op, dec = _s5_operators(
        p["ssm_lambda_re"][i], p["ssm_lambda_im"][i], p["ssm_log_step"][i], p["ssm_b_re"][i],
        p["ssm_b_im"][i], p["ssm_c_re"][i], p["ssm_c_im"][i], p["ssm_d"][i])
    return {
        "norm_mix": p["norm_mix"][i][None, :].astype(F32),
        "wq": wq[:, perm].astype(BF16), "wk": wk.astype(BF16), "wvt": wv.T.astype(BF16),
        "wu": wu.astype(BF16), "wga": wga.astype(BF16), "wgs": wgs.astype(BF16),
        "qn": two(p["q_norm"][i]), "kn": two(p["k_norm"][i]), "bd": head_avg.astype(BF16),
        "bounded": _scores_bounded(p["q_norm"][i], p["k_norm"][i]),
        "s5_a": a_op, "s5_bin": bin_op, "s5_cout": cout_op, "s5_dec": dec,
        "wp": p["w_attn_proj"][i][perm, :].astype(BF16),
        "wa": p["w_glu_a"][i].astype(BF16), "wb": p["w_glu_b"][i].astype(BF16),
        "wo": p["w_out"][i].astype(BF16),
        "norm_mlp": p["norm_mlp"][i][None, :].astype(F32),
        "w1": p["w_ff1"][i].astype(BF16), "w2": p["w_ff2"][i].astype(BF16),
    }


def _tile(n, pref):
    t = min(n, pref)
    assert n % t == 0, (n, t)
    return t


def _trunk(x, layers):
    batch, seq_len, _ = x.shape
    assert seq_len % (CHUNK * SUBLANES) == 0 and seq_len % LANES == 0
    n = batch * seq_len
    rope = _rope_tables(seq_len)
    tm = _tile(seq_len, 512)
    tq = _tile(seq_len, 512)
    tk = _tile(seq_len, 4096)
    x2 = x.reshape(n, D_MODEL)
    for lw in layers:
        q, k, vt, u, ga, gs = _in_proj(x2, lw, rope, seq_len, tm)
        attn = _attention(lw["bounded"], q, k, vt, batch, seq_len, tq, tk)
        y = _s5(u, lw, batch, seq_len)
        x2 = _mix(x2, attn, y, ga, gs, lw, tm)
        x2 = _mlp(x2, lw, _tile(seq_len, 256))
    return x2.reshape(batch, seq_len, D_MODEL)


def kernel(x_prompt, x_sample, norm_mix, w_in, q_norm, k_norm, w_attn_proj, ssm_lambda_re, ssm_lambda_im, ssm_log_step, ssm_b_re, ssm_b_im, ssm_c_re, ssm_c_im, ssm_d, w_glu_a, w_glu_b, w_out, norm_mlp, w_ff1, w_ff2):
    params = dict(norm_mix=norm_mix, w_in=w_in, q_norm=q_norm, k_norm=k_norm,
                  w_attn_proj=w_attn_proj, ssm_lambda_re=ssm_lambda_re,
                  ssm_lambda_im=ssm_lambda_im, ssm_log_step=ssm_log_step, ssm_b_re=ssm_b_re,
                  ssm_b_im=ssm_b_im, ssm_c_re=ssm_c_re, ssm_c_im=ssm_c_im, ssm_d=ssm_d,
                  w_glu_a=w_glu_a, w_glu_b=w_glu_b, w_out=w_out, norm_mlp=norm_mlp,
                  w_ff1=w_ff1, w_ff2=w_ff2)
    layers = [_layer_weights(i, params) for i in range(norm_mix.shape[0])]
    return (_trunk(x_prompt, layers), _trunk(x_sample, layers))
```

```python
import functools
import math

import jax
import jax.numpy as jnp
import numpy as np
from jax import lax
from jax.experimental import pallas as pl
from jax.experimental.pallas import tpu as pltpu

D_MODEL = 1024
GRID_W = 64
N_Q_HEADS = 16
N_KV_HEADS = 4
HEAD_DIM = 64
ROPE_THETA = 10000.0
AXIS_DIM = HEAD_DIM // 2
Q_WIDTH = N_Q_HEADS * HEAD_DIM
KV_WIDTH = N_KV_HEADS * HEAD_DIM
SSM_WIDTH = 512
SSM_GROUP = 16
N_SSM_GROUPS = SSM_WIDTH // SSM_GROUP
SSM_STATE = 64
D_FF = 4 * D_MODEL
EPS = 1e-6

LANES = 128
SUBLANES = 8
CHUNK = 8
GROUPS_PER_BLOCK = LANES // SSM_GROUP
N_CHANNEL_BLOCKS = SSM_WIDTH // LANES
CHUNK_COLS = CHUNK * LANES
STATE_PART = GROUPS_PER_BLOCK * SSM_STATE
STATE_COLS = 4 * STATE_PART
S5_ROW_BLOCK = 256
KV_PAIRS = N_KV_HEADS // 2
Q_PER_KV = N_Q_HEADS // N_KV_HEADS
VT_ROWS = KV_PAIRS * 4 * HEAD_DIM
LOG2E = math.log2(math.e)
Q_SCALE = HEAD_DIM ** -0.5 * LOG2E
MAX_UNSHIFTED_SCORE = 100.0
SHIFTED_KEY_TILE = 512
VMEM_LIMIT = 56 * 1024 * 1024

BF16 = jnp.bfloat16
F32 = jnp.float32


def _cparams(*sem):
    return pltpu.CompilerParams(dimension_semantics=sem, vmem_limit_bytes=VMEM_LIMIT)


def _in_proj_kernel(x_ref, gain_ref, wq_ref, wk_ref, wvt_ref, wu_ref, wga_ref, wgs_ref,
                    qn_ref, kn_ref, bd_ref, cos_ref, sa_ref, sb_ref,
                    q_ref, k_ref, vt_ref, u_ref, ga_ref, gs_ref):
    x = x_ref[...]
    xn = x * lax.rsqrt(jnp.mean(x * x, axis=-1, keepdims=True) + EPS) * gain_ref[...]
    xn = xn.astype(BF16)
    cos = cos_ref[...]
    sa = sa_ref[...]
    sb = sb_ref[...]
    bd = bd_ref[...]

    def head_norm_rope(blk, gain, scale):
        msq = jnp.dot((blk * blk).astype(BF16), bd, preferred_element_type=F32)
        y = blk * lax.rsqrt(msq + EPS) * gain
        y = y * cos + pltpu.roll(y, LANES - 1, 1) * sa + pltpu.roll(y, 1, 1) * sb
        return y * scale

    q = jnp.dot(xn, wq_ref[...], preferred_element_type=F32)
    for b in range(Q_WIDTH // LANES):
        sl = slice(b * LANES, (b + 1) * LANES)
        q_ref[:, sl] = head_norm_rope(q[:, sl], qn_ref[...], Q_SCALE).astype(BF16)
    k = jnp.dot(xn, wk_ref[...], preferred_element_type=F32)
    for b in range(KV_WIDTH // LANES):
        sl = slice(b * LANES, (b + 1) * LANES)
        k_ref[:, sl] = head_norm_rope(k[:, sl], kn_ref[...], 1.0).astype(BF16)
    vt = lax.dot_general(wvt_ref[...], xn, (((1,), (1,)), ((), ())), preferred_element_type=F32)
    vt = vt.astype(BF16)
    ones = jnp.ones((2 * HEAD_DIM, vt.shape[1]), BF16)
    for j in range(KV_PAIRS):
        src, dst = 2 * j * HEAD_DIM, 4 * j * HEAD_DIM
        vt_ref[dst:dst + HEAD_DIM, :] = vt[src:src + HEAD_DIM, :]
        vt_ref[dst + HEAD_DIM:dst + 3 * HEAD_DIM, :] = ones
        vt_ref[dst + 3 * HEAD_DIM:dst + 4 * HEAD_DIM, :] = vt[src + HEAD_DIM:src + 2 * HEAD_DIM, :]
    u_ref[...] = jnp.dot(xn, wu_ref[...], preferred_element_type=F32)
    ga = jnp.dot(xn, wga_ref[...], preferred_element_type=F32)
    ga_ref[...] = jax.nn.sigmoid(ga).astype(BF16)
    gs = jnp.dot(xn, wgs_ref[...], preferred_element_type=F32)
    gs_ref[...] = jax.nn.sigmoid(gs).astype(BF16)


def _in_proj(x2, lw, rope, seq_len, tm):
    n = x2.shape[0]
    tiles_per_seq = seq_len // tm
    row = lambda i: (i, 0)
    const = lambda i: (0, 0)
    pos = lambda i: (i % tiles_per_seq, 0)
    full = lambda a: pl.BlockSpec(a.shape, const)
    cos, sa, sb = rope
    in_specs = [
        pl.BlockSpec((tm, D_MODEL), row), full(lw["norm_mix"]),
        full(lw["wq"]), full(lw["wk"]), full(lw["wvt"]), full(lw["wu"]), full(lw["wga"]),
        full(lw["wgs"]), full(lw["qn"]), full(lw["kn"]), full(lw["bd"]),
        pl.BlockSpec((tm, LANES), pos), pl.BlockSpec((tm, LANES), pos),
        pl.BlockSpec((tm, LANES), pos),
    ]
    out_shape = [
        jax.ShapeDtypeStruct((n, Q_WIDTH), BF16), jax.ShapeDtypeStruct((n, KV_WIDTH), BF16),
        jax.ShapeDtypeStruct((VT_ROWS, n), BF16), jax.ShapeDtypeStruct((n, SSM_WIDTH), F32),
        jax.ShapeDtypeStruct((n, D_MODEL), BF16), jax.ShapeDtypeStruct((n, D_MODEL), BF16),
    ]
    out_specs = [
        pl.BlockSpec((tm, Q_WIDTH), row), pl.BlockSpec((tm, KV_WIDTH), row),
        pl.BlockSpec((VT_ROWS, tm), lambda i: (0, i)), pl.BlockSpec((tm, SSM_WIDTH), row),
        pl.BlockSpec((tm, D_MODEL), row), pl.BlockSpec((tm, D_MODEL), row),
    ]
    return pl.pallas_call(
        _in_proj_kernel, grid=(n // tm,), in_specs=in_specs, out_specs=out_specs,
        out_shape=out_shape, compiler_params=_cparams("parallel"), name="in_proj",
    )(x2, lw["norm_mix"], lw["wq"], lw["wk"], lw["wvt"], lw["wu"], lw["wga"], lw["wgs"],
      lw["qn"], lw["kn"], lw["bd"], cos, sa, sb)


def _attention_kernel(bounded_ref, q_ref, k_ref, vt_ref, o_ref, *, tk):
    tq = q_ref.shape[0]
    seq_len = k_ref.shape[0]
    tk_shifted = min(tk, SHIFTED_KEY_TILE)
    lane = lax.broadcasted_iota(jnp.int32, (tq, LANES), 1)
    row = lax.broadcasted_iota(jnp.int32, (LANES, tq), 0)
    nt = (((1,), (1,)), ((), ()))

    def head_pair(m):
        qm = q_ref[:, m * LANES:(m + 1) * LANES]
        return (jnp.where(lane < HEAD_DIM, qm, jnp.zeros_like(qm)),
                jnp.where(lane >= HEAD_DIM, qm, jnp.zeros_like(qm)))

    def tiles(t, width):
        start = pl.multiple_of(t * width, width)
        return (k_ref[pl.ds(start, width), :], vt_ref[0:LANES, pl.ds(start, width)],
                vt_ref[LANES:2 * LANES, pl.ds(start, width)])

    def finish(m, acc_a, acc_b):
        o_t = jnp.where(row < HEAD_DIM, acc_a / acc_a[HEAD_DIM:HEAD_DIM + 1, :],
                        acc_b / acc_b[0:1, :])
        o_ref[:, m * LANES:(m + 1) * LANES] = o_t.T.astype(BF16)

    @pl.when(bounded_ref[0] != 0)
    def _():
        for m in range(Q_PER_KV):
            q_a, q_b = head_pair(m)

            q_ab = jnp.concatenate([q_a, q_b], axis=0)

            def body(t, carry):
                kt, va, vb = tiles(t, tk)
                s = lax.dot_general(kt, q_ab, nt, preferred_element_type=F32)
                p = jnp.exp2(s).astype(BF16)
                return (carry[0] + jnp.dot(va, p[:, :tq], preferred_element_type=F32),
                        carry[1] + jnp.dot(vb, p[:, tq:], preferred_element_type=F32))

            zero = jnp.zeros((LANES, tq), F32)
            acc_a, acc_b = lax.fori_loop(0, seq_len // tk, body, (zero, zero))
            finish(m, acc_a, acc_b)

    @pl.when(bounded_ref[0] == 0)
    def _():
        for m in range(Q_PER_KV):
            q_a, q_b = head_pair(m)

            def body(t, carry):
                kt, va, vb = tiles(t, tk_shifted)
                new = []
                for qh, vh, (mx, acc) in zip((q_a, q_b), (va, vb), carry):
                    s = lax.dot_general(kt, qh, nt, preferred_element_type=F32)
                    mx_new = jnp.maximum(mx, jnp.max(s, axis=0, keepdims=True))
                    p = jnp.exp2(s - mx_new).astype(BF16)
                    acc_new = (jnp.exp2(mx - mx_new) * acc
                               + jnp.dot(vh, p, preferred_element_type=F32))
                    new.append((mx_new, acc_new))
                return tuple(new)

            init = tuple((jnp.full((1, tq), -jnp.inf, F32), jnp.zeros((LANES, tq), F32))
                         for _ in range(2))
            (_, acc_a), (_, acc_b) = lax.fori_loop(0, seq_len // tk_shifted, body, init)
            finish(m, acc_a, acc_b)


def _attention(bounded, q, k, vt, batch, seq_len, tq, tk):
    n = q.shape[0]
    q_tiles = seq_len // tq
    pair_w = Q_PER_KV * LANES
    qmap = lambda b, j, i: (b * q_tiles + i, j)
    return pl.pallas_call(
        functools.partial(_attention_kernel, tk=tk),
        grid=(batch, KV_PAIRS, q_tiles),
        in_specs=[pl.BlockSpec(memory_space=pltpu.SMEM),
                  pl.BlockSpec((tq, pair_w), qmap),
                  pl.BlockSpec((seq_len, LANES), lambda b, j, i: (b, j)),
                  pl.BlockSpec((2 * LANES, seq_len), lambda b, j, i: (j, b))],
        out_specs=pl.BlockSpec((tq, pair_w), qmap),
        out_shape=jax.ShapeDtypeStruct((n, Q_WIDTH), BF16),
        compiler_params=_cparams("parallel", "parallel", "parallel"), name="attention",
    )(bounded, q, k, vt)


def _cmul(ar, ai, xr, xi):
    return ar * xr - ai * xi, ar * xi + ai * xr


def _s5_kernel(u_ref, a_ref, bin_ref, cout_ref, dec_ref, y_ref, x_ref, s_ref, ya_ref):
    n_chunks = x_ref.shape[0]
    n_tiles = n_chunks // SUBLANES
    row_block = min(S5_ROW_BLOCK, n_chunks)
    for i in range(CHUNK):
        x_ref[:, i * LANES:(i + 1) * LANES] = (
            u_ref[pl.ds(i, n_chunks, stride=CHUNK), :].astype(BF16))
    for r in range(n_chunks // row_block):
        rs = slice(r * row_block, (r + 1) * row_block)
        s_ref[rs, :] = jnp.dot(x_ref[rs, :], bin_ref[0], preferred_element_type=F32)
    f_re, f_im = slice(0, STATE_PART), slice(STATE_PART, 2 * STATE_PART)
    b_re, b_im = slice(2 * STATE_PART, 3 * STATE_PART), slice(3 * STATE_PART, 4 * STATE_PART)
    rows = lax.broadcasted_iota(jnp.int32, (SUBLANES, STATE_PART), 0)

    def scan_tile(xr, xi, cr, ci, re, im, down):
        for k in range(3):
            shift = (1 << k) if down else SUBLANES - (1 << k)
            pr, pi = _cmul(dec_ref[0, k, :, re], dec_ref[0, k, :, im],
                           pltpu.roll(xr, shift, 0), pltpu.roll(xi, shift, 0))
            xr, xi = xr + pr, xi + pi
        one = 1 if down else SUBLANES - 1
        edge = 0 if down else SUBLANES - 1
        er, ei = _cmul(dec_ref[0, 3, :, re], dec_ref[0, 3, :, im], cr, ci)
        er = er + jnp.where(rows == edge, 0.0, pltpu.roll(xr, one, 0))
        ei = ei + jnp.where(rows == edge, 0.0, pltpu.roll(xi, one, 0))
        last = SUBLANES - 1 - edge
        nr, ni = _cmul(dec_ref[0, 4, 0:1, re], dec_ref[0, 4, 0:1, im], cr, ci)
        return er, ei, nr + xr[last:last + 1, :], ni + xi[last:last + 1, :]

    for r in range(n_chunks // row_block):
        rs = slice(r * row_block, (r + 1) * row_block)
        ya_ref[rs, :] = jnp.dot(x_ref[rs, :], a_ref[0], preferred_element_type=F32)
    hr = hi = gr = gi = jnp.zeros((1, STATE_PART), F32)
    for t in range(n_tiles):
        rf = slice(t * SUBLANES, (t + 1) * SUBLANES)
        rb = slice((n_tiles - 1 - t) * SUBLANES, (n_tiles - t) * SUBLANES)
        er, ei, hr, hi = scan_tile(s_ref[rf, f_re], s_ref[rf, f_im], hr, hi, f_re, f_im, True)
        s_ref[rf, f_re] = er
        s_ref[rf, f_im] = ei
        er, ei, gr, gi = scan_tile(s_ref[rb, b_re], s_ref[rb, b_im], gr, gi, b_re, b_im, False)
        s_ref[rb, b_re] = er
        s_ref[rb, b_im] = ei
    for r in range(n_chunks // row_block):
        rs = slice(r * row_block, (r + 1) * row_block)
        y = ya_ref[rs, :] + jnp.dot(s_ref[rs, :].astype(BF16), cout_ref[0],
                                    preferred_element_type=F32)
        for i in range(CHUNK):
            y_ref[pl.ds(r * row_block * CHUNK + i, row_block, stride=CHUNK), :] = (
                y[:, i * LANES:(i + 1) * LANES])


def _s5(u, lw, batch, seq_len):
    n_chunks = seq_len // CHUNK
    w3 = lambda b, s: (b, 0, 0)
    io = pl.BlockSpec((seq_len, LANES), lambda b, s: (s, b))
    return pl.pallas_call(
        _s5_kernel, grid=(N_CHANNEL_BLOCKS, batch),
        in_specs=[io,
                  pl.BlockSpec((1, CHUNK_COLS, CHUNK_COLS), w3),
                  pl.BlockSpec((1, CHUNK_COLS, STATE_COLS), w3),
                  pl.BlockSpec((1, STATE_COLS, CHUNK_COLS), w3),
                  pl.BlockSpec((1, 5, SUBLANES, STATE_COLS), lambda b, s: (b, 0, 0, 0))],
        out_specs=io,
        out_shape=jax.ShapeDtypeStruct(u.shape, F32),
        scratch_shapes=[pltpu.VMEM((n_chunks, CHUNK_COLS), BF16),
                        pltpu.VMEM((n_chunks, STATE_COLS), F32),
                        pltpu.VMEM((n_chunks, CHUNK_COLS), F32)],
        compiler_params=_cparams("parallel", "parallel"), name="s5",
    )(u, lw["s5_a"], lw["s5_bin"], lw["s5_cout"], lw["s5_dec"])


def _s5_operators(lam_re, lam_im, log_step, b_re, b_im, c_re, c_im, d_skip):
    t = CHUNK
    lam = lax.complex(lam_re.astype(F32), lam_im.astype(F32))
    step = jnp.exp(log_step.astype(F32))
    lam_dt = lam * step[..., None]
    lam_bar = jnp.exp(lam_dt)
    b_bar = ((lam_bar - 1.0) / lam)[..., None] * lax.complex(b_re.astype(F32), b_im.astype(F32))
    c = lax.complex(c_re.astype(F32), c_im.astype(F32))
    powers = jnp.exp(lam_dt[None] * jnp.arange(t + 1, dtype=F32)[:, None, None, None])
    taps = jnp.einsum("dgop,ndgp,dgpi->ndgoi", c, powers[:t], b_bar).real
    idx = jnp.arange(t)
    lag = idx[None, :] - idx[:, None]
    fwd = jnp.where((lag >= 0)[:, :, None, None, None], taps[jnp.clip(lag, 0, t - 1), 0], 0.0)
    bwd = jnp.where((lag <= 0)[:, :, None, None, None], taps[jnp.clip(-lag, 0, t - 1), 1], 0.0)
    k_ji = fwd + bwd
    eye_t = jnp.eye(t, dtype=F32)
    eye_c = jnp.eye(SSM_GROUP, dtype=F32)
    skip = (eye_t[:, :, None, None, None] * eye_c[None, None, None]
            * d_skip.astype(F32).reshape(N_SSM_GROUPS, SSM_GROUP)[None, None, :, :, None])
    nb, gb = N_CHANNEL_BLOCKS, GROUPS_PER_BLOCK
    same_group = jnp.eye(gb, dtype=bool)[None, None, :, None, None, :, None]

    def block_diagonal(z, rows, cols):
        z = z.astype(BF16).transpose(0, 2, 1, 3, 4, 5)[:, :, :, :, :, None, :]
        return jnp.where(same_group, z, jnp.zeros((), BF16)).reshape(nb, rows, cols)

    a_grp = (k_ji + skip).transpose(2, 0, 4, 1, 3).reshape(nb, gb, t, SSM_GROUP, t, SSM_GROUP)
    a_op = block_diagonal(a_grp, CHUNK_COLS, CHUNK_COLS)
    in_f = jnp.einsum("jgp,gpi->gjip", powers[t - 1 - idx, 0], b_bar[0])
    in_b = jnp.einsum("jgp,gpi->gjip", powers[idx, 1], b_bar[1])
    bin_grp = jnp.stack([in_f.real, in_f.imag, in_b.real, in_b.imag], axis=3)
    bin_op = block_diagonal(bin_grp.reshape(nb, gb, t, SSM_GROUP, 4, SSM_STATE),
                            CHUNK_COLS, STATE_COLS)
    out_f = jnp.einsum("gop,igp->gpio", c[0], powers[idx + 1, 0])
    out_b = jnp.einsum("gop,igp->gpio", c[1], powers[t - idx, 1])
    cout_grp = jnp.stack([out_f.real, -out_f.imag, out_b.real, -out_b.imag], axis=1)
    cout_op = block_diagonal(cout_grp.reshape(nb, gb, 4, SSM_STATE, t, SSM_GROUP),
                             STATE_COLS, CHUNK_COLS)
    r = jnp.arange(SUBLANES)
    tile_pow = jnp.exp(lam_dt[None] * (t * jnp.arange(SUBLANES + 1, dtype=F32))[:, None, None, None])

    def scan_consts(d, down):
        z = []
        for k in (1, 2, 4):
            keep = (r >= k) if down else (r <= SUBLANES - 1 - k)
            z.append(jnp.where(keep[:, None, None], tile_pow[k, d][None], 0.0))
        z.append(tile_pow[r if down else SUBLANES - 1 - r, d])
        z.append(jnp.broadcast_to(tile_pow[SUBLANES, d][None], z[0].shape))
        return jnp.stack(z)

    zf, zb = scan_consts(0, True), scan_consts(1, False)
    dec = jnp.stack([zf.real, zf.imag, zb.real, zb.imag], axis=2)
    dec = dec.reshape(5, SUBLANES, 4, nb, gb * SSM_STATE).transpose(3, 0, 1, 2, 4)
    dec = dec.reshape(nb, 5, SUBLANES, STATE_COLS)
    return a_op, bin_op, cout_op, dec


def _mix_kernel(x_ref, attn_ref, y_ref, ga_ref, gs_ref, wp_ref, wa_ref, wb_ref, wo_ref, o_ref):
    attn_out = jnp.dot(attn_ref[...], wp_ref[...], preferred_element_type=F32)
    yg = jax.nn.gelu(y_ref[...]).astype(BF16)
    ssm_out = (jnp.dot(yg, wa_ref[...], preferred_element_type=F32)
               * jax.nn.sigmoid(jnp.dot(yg, wb_ref[...], preferred_element_type=F32)))
    merged = ga_ref[...].astype(F32) * attn_out + gs_ref[...].astype(F32) * ssm_out
    o_ref[...] = x_ref[...] + jnp.dot(merged.astype(BF16), wo_ref[...],
                                      preferred_element_type=F32)


def _mix(x2, attn, y, ga, gs, lw, tm):
    n = x2.shape[0]
    row = lambda i: (i, 0)
    full = lambda a: pl.BlockSpec(a.shape, lambda i: (0, 0))
    return pl.pallas_call(
        _mix_kernel, grid=(n // tm,),
        in_specs=[pl.BlockSpec((tm, D_MODEL), row), pl.BlockSpec((tm, Q_WIDTH), row),
                  pl.BlockSpec((tm, SSM_WIDTH), row), pl.BlockSpec((tm, D_MODEL), row),
                  pl.BlockSpec((tm, D_MODEL), row), full(lw["wp"]), full(lw["wa"]),
                  full(lw["wb"]), full(lw["wo"])],
        out_specs=pl.BlockSpec((tm, D_MODEL), row),
        out_shape=jax.ShapeDtypeStruct((n, D_MODEL), F32),
        compiler_params=_cparams("parallel"), name="mix",
    )(x2, attn, y, ga, gs, lw["wp"], lw["wa"], lw["wb"], lw["wo"])


def _mlp_kernel(x_ref, gain_ref, w1_ref, w2_ref, o_ref):
    x = x_ref[...]
    h = x * lax.rsqrt(jnp.mean(x * x, axis=-1, keepdims=True) + EPS) * gain_ref[...]
    a = jnp.dot(h.astype(BF16), w1_ref[...], preferred_element_type=F32)
    a = jnp.square(jnp.maximum(a, 0.0)).astype(BF16)
    o_ref[...] = x + jnp.dot(a, w2_ref[...], preferred_element_type=F32)


def _mlp(x2, lw, tm):
    n = x2.shape[0]
    row = lambda i: (i, 0)
    full = lambda a: pl.BlockSpec(a.shape, lambda i: (0, 0))
    return pl.pallas_call(
        _mlp_kernel, grid=(n // tm,),
        in_specs=[pl.BlockSpec((tm, D_MODEL), row), full(lw["norm_mlp"]), full(lw["w1"]),
                  full(lw["w2"])],
        out_specs=pl.BlockSpec((tm, D_MODEL), row),
        out_shape=jax.ShapeDtypeStruct((n, D_MODEL), F32),
        compiler_params=_cparams("parallel"), name="mlp",
    )(x2, lw["norm_mlp"], lw["w1"], lw["w2"])


def _q_permutation():
    perm = []
    for j in range(KV_PAIRS):
        for m in range(Q_PER_KV):
            for half in range(2):
                head = Q_PER_KV * (2 * j + half) + m
                perm.extend(range(head * HEAD_DIM, (head + 1) * HEAD_DIM))
    return np.asarray(perm, dtype=np.int32)


def _rope_tables(seq_len):
    t = jnp.arange(seq_len)
    inv_freq = ROPE_THETA ** (-jnp.arange(0, AXIS_DIM, 2, dtype=F32) / AXIS_DIM)
    ang = jnp.concatenate([(t // GRID_W).astype(F32)[:, None] * inv_freq[None, :],
                           (t % GRID_W).astype(F32)[:, None] * inv_freq[None, :]], axis=-1)
    cos = jnp.repeat(jnp.cos(ang), 2, axis=-1)
    sin = jnp.repeat(jnp.sin(ang), 2, axis=-1)
    even = (jnp.arange(HEAD_DIM) % 2 == 0)[None, :]
    sa = jnp.where(even, -sin, 0.0)
    sb = jnp.where(even, 0.0, sin)
    two = lambda a: jnp.concatenate([a, a], axis=-1)
    return two(cos), two(sa), two(sb)


def _scores_bounded(q_gain, k_gain):
    bound = (1.05 * HEAD_DIM * Q_SCALE * jnp.max(jnp.abs(q_gain.astype(F32)))
             * jnp.max(jnp.abs(k_gain.astype(F32))))
    return (bound <= MAX_UNSHIFTED_SCORE).astype(jnp.int32).reshape(1)


def _layer_weights(i, p):
    perm = _q_permutation()
    w_in = p["w_in"][i]
    o = 0
    wq = w_in[:, o:o + Q_WIDTH]; o += Q_WIDTH
    wk = w_in[:, o:o + KV_WIDTH]; o += KV_WIDTH
    wv = w_in[:, o:o + KV_WIDTH]; o += KV_WIDTH
    wu = w_in[:, o:o + SSM_WIDTH]; o += SSM_WIDTH
    wga = w_in[:, o:o + D_MODEL]; o += D_MODEL
    wgs = w_in[:, o:o + D_MODEL]
    head_avg = jnp.kron(jnp.eye(LANES // HEAD_DIM, dtype=F32),
                        jnp.full((HEAD_DIM, HEAD_DIM), 1.0 / HEAD_DIM, F32))
    two = lambda a: jnp.concatenate([a, a])[None, :].astype(F32)
    a_op, bin_op, cout_op, dec = _s5_operators(
        p["ssm_lambda_re"][i], p["ssm_lambda_im"][i], p["ssm_log_step"][i], p["ssm_b_re"][i],
        p["ssm_b_im"][i], p["ssm_c_re"][i], p["ssm_c_im"][i], p["ssm_d"][i])
    return {
        "norm_mix": p["norm_mix"][i][None, :].astype(F32),
        "wq": wq[:, perm].astype(BF16), "wk": wk.astype(BF16), "wvt": wv.T.astype(BF16),
        "wu": wu.astype(BF16), "wga": wga.astype(BF16), "wgs": wgs.astype(BF16),
        "qn": two(p["q_norm"][i]), "kn": two(p["k_norm"][i]), "bd": head_avg.astype(BF16),
        "bounded": _scores_bounded(p["q_norm"][i], p["k_norm"][i]),
        "s5_a": a_op, "s5_bin": bin_op, "s5_cout": cout_op, "s5_dec": dec,
        "wp": p["w_attn_proj"][i][perm, :].astype(BF16),
        "wa": p["w_glu_a"][i].astype(BF16), "wb": p["w_glu_b"][i].astype(BF16),
        "wo": p["w_out"][i].astype(BF16),
        "norm_mlp": p["norm_mlp"][i][None, :].astype(F32),
        "w1": p["w_ff1"][i].astype(BF16), "w2": p["w_ff2"][i].astype(BF16),
    }


def _tile(n, pref):
    t = min(n, pref)
    assert n % t == 0, (n, t)
    return t


def _trunk(x, layers):
    batch, seq_len, _ = x.shape
    assert seq_len % (CHUNK * SUBLANES) == 0 and seq_len % LANES == 0
    n = batch * seq_len
    rope = _rope_tables(seq_len)
    tm = _tile(seq_len, 512)
    tq = _tile(seq_len, 512)
    tk = _tile(seq_len, 4096)
    x2 = x.reshape(n, D_MODEL)
    for lw in layers:
        q, k, vt, u, ga, gs = _in_proj(x2, lw, rope, seq_len, tm)
        attn = _attention(lw["bounded"], q, k, vt, batch, seq_len, tq, tk)
        y = _s5(u, lw, batch, seq_len)
        x2 = _mix(x2, attn, y, ga, gs, lw, tm)
        x2 = _mlp(x2, lw, _tile(seq_len, 256))
    return x2.reshape(batch, seq_len, D_MODEL)


def kernel(x_prompt, x_sample, norm_mix, w_in, q_norm, k_norm, w_attn_proj, ssm_lambda_re, ssm_lambda_im, ssm_log_step, ssm_b_re, ssm_b_im, ssm_c_re, ssm_c_im, ssm_d, w_glu_a, w_glu_b, w_out, norm_mlp, w_ff1, w_ff2):
    params = dict(norm_mix=norm_mix, w_in=w_in, q_norm=q_norm, k_norm=k_norm,
                  w_attn_proj=w_attn_proj, ssm_lambda_re=ssm_lambda_re,
                  ssm_lambda_im=ssm_lambda_im, ssm_log_step=ssm_log_step, ssm_b_re=ssm_b_re,
                  ssm_b_im=ssm_b_im, ssm_c_re=ssm_c_re, ssm_c_im=ssm_c_im, ssm_d=ssm_d,
                  w_glu_a=w_glu_a, w_glu_b=w_glu_b, w_out=w_out, norm_mlp=norm_mlp,
                  w_ff1=w_ff1, w_ff2=w_ff2)
    layers = [_layer_weights(i, params) for i in range(norm_mix.shape[0])]
    return (_trunk(x_prompt, layers), _trunk(x_sample, layers))
```

```python
import functools
import math

import jax
import jax.numpy as jnp
import numpy as np
from jax import lax
from jax.experimental import pallas as pl
from jax.experimental.pallas import tpu as pltpu

D_MODEL = 1024
GRID_W = 64
N_Q_HEADS = 16
N_KV_HEADS = 4
HEAD_DIM = 64
ROPE_THETA = 10000.0
AXIS_DIM = HEAD_DIM // 2
Q_WIDTH = N_Q_HEADS * HEAD_DIM
KV_WIDTH = N_KV_HEADS * HEAD_DIM
SSM_WIDTH = 512
SSM_GROUP = 16
N_SSM_GROUPS = SSM_WIDTH // SSM_GROUP
SSM_STATE = 64
D_FF = 4 * D_MODEL
EPS = 1e-6

LANES = 128
SUBLANES = 8
CHUNK = 8
GROUPS_PER_BLOCK = LANES // SSM_GROUP
N_CHANNEL_BLOCKS = SSM_WIDTH // LANES
CHUNK_COLS = CHUNK * LANES
STATE_PART = GROUPS_PER_BLOCK * SSM_STATE
STATE_COLS = 4 * STATE_PART
S5_ROW_BLOCK = 256
KV_PAIRS = N_KV_HEADS // 2
Q_PER_KV = N_Q_HEADS // N_KV_HEADS
VT_ROWS = KV_PAIRS * 4 * HEAD_DIM
LOG2E = math.log2(math.e)
Q_SCALE = HEAD_DIM ** -0.5 * LOG2E
MAX_UNSHIFTED_SCORE = 100.0
SHIFTED_KEY_TILE = 512
VMEM_LIMIT = 56 * 1024 * 1024

BF16 = jnp.bfloat16
F32 = jnp.float32


def _cparams(*sem):
    return pltpu.CompilerParams(dimension_semantics=sem, vmem_limit_bytes=VMEM_LIMIT)


def _in_proj_kernel(x_ref, gain_ref, wq_ref, wk_ref, wvt_ref, wu_ref, wga_ref, wgs_ref,
                    qn_ref, kn_ref, bd_ref, cos_ref, sa_ref, sb_ref,
                    q_ref, k_ref, vt_ref, u_ref, ga_ref, gs_ref):
    x = x_ref[...]
    xn = x * lax.rsqrt(jnp.mean(x * x, axis=-1, keepdims=True) + EPS) * gain_ref[...]
    xn = xn.astype(BF16)
    cos = cos_ref[...]
    sa = sa_ref[...]
    sb = sb_ref[...]
    bd = bd_ref[...]

    def head_norm_rope(blk, gain, scale):
        msq = jnp.dot((blk * blk).astype(BF16), bd, preferred_element_type=F32)
        y = blk * lax.rsqrt(msq + EPS) * gain
        y = y * cos + pltpu.roll(y, LANES - 1, 1) * sa + pltpu.roll(y, 1, 1) * sb
        return y * scale

    q = jnp.dot(xn, wq_ref[...], preferred_element_type=F32)
    for b in range(Q_WIDTH // LANES):
        sl = slice(b * LANES, (b + 1) * LANES)
        q_ref[:, sl] = head_norm_rope(q[:, sl], qn_ref[...], Q_SCALE).astype(BF16)
    k = jnp.dot(xn, wk_ref[...], preferred_element_type=F32)
    for b in range(KV_WIDTH // LANES):
        sl = slice(b * LANES, (b + 1) * LANES)
        k_ref[:, sl] = head_norm_rope(k[:, sl], kn_ref[...], 1.0).astype(BF16)
    vt = lax.dot_general(wvt_ref[...], xn, (((1,), (1,)), ((), ())), preferred_element_type=F32)
    vt = vt.astype(BF16)
    ones = jnp.ones((2 * HEAD_DIM, vt.shape[1]), BF16)
    for j in range(KV_PAIRS):
        src, dst = 2 * j * HEAD_DIM, 4 * j * HEAD_DIM
        vt_ref[dst:dst + HEAD_DIM, :] = vt[src:src + HEAD_DIM, :]
        vt_ref[dst + HEAD_DIM:dst + 3 * HEAD_DIM, :] = ones
        vt_ref[dst + 3 * HEAD_DIM:dst + 4 * HEAD_DIM, :] = vt[src + HEAD_DIM:src + 2 * HEAD_DIM, :]
    u_ref[...] = jnp.dot(xn, wu_ref[...], preferred_element_type=F32)
    ga = jnp.dot(xn, wga_ref[...], preferred_element_type=F32)
    ga_ref[...] = jax.nn.sigmoid(ga).astype(BF16)
    gs = jnp.dot(xn, wgs_ref[...], preferred_element_type=F32)
    gs_ref[...] = jax.nn.sigmoid(gs).astype(BF16)


def _in_proj(x2, lw, rope, seq_len, tm):
    n = x2.shape[0]
    tiles_per_seq = seq_len // tm
    row = lambda i: (i, 0)
    const = lambda i: (0, 0)
    pos = lambda i: (i % tiles_per_seq, 0)
    full = lambda a: pl.BlockSpec(a.shape, const)
    cos, sa, sb = rope
    in_specs = [
        pl.BlockSpec((tm, D_MODEL), row), full(lw["norm_mix"]),
        full(lw["wq"]), full(lw["wk"]), full(lw["wvt"]), full(lw["wu"]), full(lw["wga"]),
        full(lw["wgs"]), full(lw["qn"]), full(lw["kn"]), full(lw["bd"]),
        pl.BlockSpec((tm, LANES), pos), pl.BlockSpec((tm, LANES), pos),
        pl.BlockSpec((tm, LANES), pos),
    ]
    out_shape = [
        jax.ShapeDtypeStruct((n, Q_WIDTH), BF16), jax.ShapeDtypeStruct((n, KV_WIDTH), BF16),
        jax.ShapeDtypeStruct((VT_ROWS, n), BF16), jax.ShapeDtypeStruct((n, SSM_WIDTH), F32),
        jax.ShapeDtypeStruct((n, D_MODEL), BF16), jax.ShapeDtypeStruct((n, D_MODEL), BF16),
    ]
    out_specs = [
        pl.BlockSpec((tm, Q_WIDTH), row), pl.BlockSpec((tm, KV_WIDTH), row),
        pl.BlockSpec((VT_ROWS, tm), lambda i: (0, i)), pl.BlockSpec((tm, SSM_WIDTH), row),
        pl.BlockSpec((tm, D_MODEL), row), pl.BlockSpec((tm, D_MODEL), row),
    ]
    return pl.pallas_call(
        _in_proj_kernel, grid=(n // tm,), in_specs=in_specs, out_specs=out_specs,
        out_shape=out_shape, compiler_params=_cparams("parallel"), name="in_proj",
    )(x2, lw["norm_mix"], lw["wq"], lw["wk"], lw["wvt"], lw["wu"], lw["wga"], lw["wgs"],
      lw["qn"], lw["kn"], lw["bd"], cos, sa, sb)


def _attention_kernel(bounded_ref, q_ref, k_ref, vt_ref, o_ref, *, tk):
    tq = q_ref.shape[0]
    seq_len = k_ref.shape[0]
    tk_shifted = min(tk, SHIFTED_KEY_TILE)
    lane = lax.broadcasted_iota(jnp.int32, (tq, LANES), 1)
    row = lax.broadcasted_iota(jnp.int32, (LANES, tq), 0)
    nt = (((1,), (1,)), ((), ()))

    def head_pair(m):
        qm = q_ref[:, m * LANES:(m + 1) * LANES]
        return (jnp.where(lane < HEAD_DIM, qm, jnp.zeros_like(qm)),
                jnp.where(lane >= HEAD_DIM, qm, jnp.zeros_like(qm)))

    def tiles(t, width):
        start = pl.multiple_of(t * width, width)
        return (k_ref[pl.ds(start, width), :], vt_ref[0:LANES, pl.ds(start, width)],
                vt_ref[LANES:2 * LANES, pl.ds(start, width)])

    def finish(m, acc_a, acc_b):
        o_t = jnp.where(row < HEAD_DIM, acc_a / acc_a[HEAD_DIM:HEAD_DIM + 1, :],
                        acc_b / acc_b[0:1, :])
        o_ref[:, m * LANES:(m + 1) * LANES] = o_t.T.astype(BF16)

    @pl.when(bounded_ref[0] != 0)
    def _():
        for m in range(Q_PER_KV):
            q_a, q_b = head_pair(m)

            q_ab = jnp.concatenate([q_a, q_b], axis=0)

            def body(t, carry):
                kt, va, vb = tiles(t, tk)
                s = lax.dot_general(kt, q_ab, nt, preferred_element_type=F32)
                p = jnp.exp2(s).astype(BF16)
                return (carry[0] + jnp.dot(va, p[:, :tq], preferred_element_type=F32),
                        carry[1] + jnp.dot(vb, p[:, tq:], preferred_element_type=F32))

            zero = jnp.zeros((LANES, tq), F32)
            acc_a, acc_b = lax.fori_loop(0, seq_len // tk, body, (zero, zero))
            finish(m, acc_a, acc_b)

    @pl.when(bounded_ref[0] == 0)
    def _():
        for m in range(Q_PER_KV):
            q_a, q_b = head_pair(m)

            def body(t, carry):
                kt, va, vb = tiles(t, tk_shifted)
                new = []
                for qh, vh, (mx, acc) in zip((q_a, q_b), (va, vb), carry):
                    s = lax.dot_general(kt, qh, nt, preferred_element_type=F32)
                    mx_new = jnp.maximum(mx, jnp.max(s, axis=0, keepdims=True))
                    p = jnp.exp2(s - mx_new).astype(BF16)
                    acc_new = (jnp.exp2(mx - mx_new) * acc
                               + jnp.dot(vh, p, preferred_element_type=F32))
                    new.append((mx_new, acc_new))
                return tuple(new)

            init = tuple((jnp.full((1, tq), -jnp.inf, F32), jnp.zeros((LANES, tq), F32))
                         for _ in range(2))
            (_, acc_a), (_, acc_b) = lax.fori_loop(0, seq_len // tk_shifted, body, init)
            finish(m, acc_a, acc_b)


def _attention(bounded, q, k, vt, batch, seq_len, tq, tk):
    n = q.shape[0]
    q_tiles = seq_len // tq
    pair_w = Q_PER_KV * LANES
    qmap = lambda b, j, i: (b * q_tiles + i, j)
    return pl.pallas_call(
        functools.partial(_attention_kernel, tk=tk),
        grid=(batch, KV_PAIRS, q_tiles),
        in_specs=[pl.BlockSpec(memory_space=pltpu.SMEM),
                  pl.BlockSpec((tq, pair_w), qmap),
                  pl.BlockSpec((seq_len, LANES), lambda b, j, i: (b, j)),
                  pl.BlockSpec((2 * LANES, seq_len), lambda b, j, i: (j, b))],
        out_specs=pl.BlockSpec((tq, pair_w), qmap),
        out_shape=jax.ShapeDtypeStruct((n, Q_WIDTH), BF16),
        compiler_params=_cparams("parallel", "parallel", "parallel"), name="attention",
    )(bounded, q, k, vt)


def _cmul(ar, ai, xr, xi):
    return ar * xr - ai * xi, ar * xi + ai * xr


def _s5_kernel(u_ref, a_ref, bin_ref, cout_ref, dec_ref, y_ref, x_ref, s_ref, ya_ref):
    n_chunks = x_ref.shape[0]
    n_tiles = n_chunks // SUBLANES
    row_block = min(S5_ROW_BLOCK, n_chunks)
    for i in range(CHUNK):
        x_ref[:, i * LANES:(i + 1) * LANES] = (
            u_ref[pl.ds(i, n_chunks, stride=CHUNK), :].astype(BF16))
    for r in range(n_chunks // row_block):
        rs = slice(r * row_block, (r + 1) * row_block)
        s_ref[rs, :] = jnp.dot(x_ref[rs, :], bin_ref[0], preferred_element_type=F32)
    f_re, f_im = slice(0, STATE_PART), slice(STATE_PART, 2 * STATE_PART)
    b_re, b_im = slice(2 * STATE_PART, 3 * STATE_PART), slice(3 * STATE_PART, 4 * STATE_PART)
    rows = lax.broadcasted_iota(jnp.int32, (SUBLANES, STATE_PART), 0)

    def scan_tile(xr, xi, cr, ci, re, im, down):
        for k in range(3):
            shift = (1 << k) if down else SUBLANES - (1 << k)
            pr, pi = _cmul(dec_ref[0, k, :, re], dec_ref[0, k, :, im],
                           pltpu.roll(xr, shift, 0), pltpu.roll(xi, shift, 0))
            xr, xi = xr + pr, xi + pi
        one = 1 if down else SUBLANES - 1
        edge = 0 if down else SUBLANES - 1
        er, ei = _cmul(dec_ref[0, 3, :, re], dec_ref[0, 3, :, im], cr, ci)
        er = er + jnp.where(rows == edge, 0.0, pltpu.roll(xr, one, 0))
        ei = ei + jnp.where(rows == edge, 0.0, pltpu.roll(xi, one, 0))
        last = SUBLANES - 1 - edge
        nr, ni = _cmul(dec_ref[0, 4, 0:1, re], dec_ref[0, 4, 0:1, im], cr, ci)
        return er, ei, nr + xr[last:last + 1, :], ni + xi[last:last + 1, :]

    for r in range(n_chunks // row_block):
        rs = slice(r * row_block, (r + 1) * row_block)
        ya_ref[rs, :] = jnp.dot(x_ref[rs, :], a_ref[0], preferred_element_type=F32)
    hr = hi = gr = gi = jnp.zeros((1, STATE_PART), F32)
    for t in range(n_tiles):
        rf = slice(t * SUBLANES, (t + 1) * SUBLANES)
        rb = slice((n_tiles - 1 - t) * SUBLANES, (n_tiles - t) * SUBLANES)
        er, ei, hr, hi = scan_tile(s_ref[rf, f_re], s_ref[rf, f_im], hr, hi, f_re, f_im, True)
        s_ref[rf, f_re] = er
        s_ref[rf, f_im] = ei
        er, ei, gr, gi = scan_tile(s_ref[rb, b_re], s_ref[rb, b_im], gr, gi, b_re, b_im, False)
        s_ref[rb, b_re] = er
        s_ref[rb, b_im] = ei
    for r in range(n_chunks // row_block):
        rs = slice(r * row_block, (r + 1) * row_block)
        y = ya_ref[rs, :] + jnp.dot(s_ref[rs, :].astype(BF16), cout_ref[0],
                                    preferred_element_type=F32)
        for i in range(CHUNK):
            y_ref[pl.ds(r * row_block * CHUNK + i, row_block, stride=CHUNK), :] = (
                y[:, i * LANES:(i + 1) * LANES])


def _s5(u, lw, batch, seq_len):
    n_chunks = seq_len // CHUNK
    w3 = lambda b, s: (b, 0, 0)
    io = pl.BlockSpec((seq_len, LANES), lambda b, s: (s, b))
    return pl.pallas_call(
        _s5_kernel, grid=(N_CHANNEL_BLOCKS, batch),
        in_specs=[io,
                  pl.BlockSpec((1, CHUNK_COLS, CHUNK_COLS), w3),
                  pl.BlockSpec((1, CHUNK_COLS, STATE_COLS), w3),
                  pl.BlockSpec((1, STATE_COLS, CHUNK_COLS), w3),
                  pl.BlockSpec((1, 5, SUBLANES, STATE_COLS), lambda b, s: (b, 0, 0, 0))],
        out_specs=io,
        out_shape=jax.ShapeDtypeStruct(u.shape, F32),
        scratch_shapes=[pltpu.VMEM((n_chunks, CHUNK_COLS), BF16),
                        pltpu.VMEM((n_chunks, STATE_COLS), F32),
                        pltpu.VMEM((n_chunks, CHUNK_COLS), F32)],
        compiler_params=_cparams("parallel", "parallel"), name="s5",
    )(u, lw["s5_a"], lw["s5_bin"], lw["s5_cout"], lw["s5_dec"])


def _s5_operators(lam_re, lam_im, log_step, b_re, b_im, c_re, c_im, d_skip):
    t = CHUNK
    lam = lax.complex(lam_re.astype(F32), lam_im.astype(F32))
    step = jnp.exp(log_step.astype(F32))
    lam_dt = lam * step[..., None]
    lam_bar = jnp.exp(lam_dt)
    b_bar = ((lam_bar - 1.0) / lam)[..., None] * lax.complex(b_re.astype(F32), b_im.astype(F32))
    c = lax.complex(c_re.astype(F32), c_im.astype(F32))
    powers = jnp.exp(lam_dt[None] * jnp.arange(t + 1, dtype=F32)[:, None, None, None])
    taps = jnp.einsum("dgop,ndgp,dgpi->ndgoi", c, powers[:t], b_bar).real
    idx = jnp.arange(t)
    lag = idx[None, :] - idx[:, None]
    fwd = jnp.where((lag >= 0)[:, :, None, None, None], taps[jnp.clip(lag, 0, t - 1), 0], 0.0)
    bwd = jnp.where((lag <= 0)[:, :, None, None, None], taps[jnp.clip(-lag, 0, t - 1), 1], 0.0)
    k_ji = fwd + bwd
    eye_t = jnp.eye(t, dtype=F32)
    eye_c = jnp.eye(SSM_GROUP, dtype=F32)
    skip = (eye_t[:, :, None, None, None] * eye_c[None, None, None]
            * d_skip.astype(F32).reshape(N_SSM_GROUPS, SSM_GROUP)[None, None, :, :, None])
    nb, gb = N_CHANNEL_BLOCKS, GROUPS_PER_BLOCK

    def block_diagonal(z, rows, cols):
        r2, c1, c2 = z.shape[3], z.shape[4], z.shape[5]
        compact = z.astype(BF16).transpose(0, 2, 1, 3, 4, 5).reshape(nb, rows, c1 * c2)
        src = np.arange(cols) // (gb * c2) * c2 + np.arange(cols) % c2
        spread = jnp.asarray(np.arange(c1 * c2)[:, None] == src[None, :], BF16)
        wide = jnp.einsum("brk,kc->brc", compact, spread, preferred_element_type=F32)
        row_g = lax.broadcasted_iota(jnp.int32, (rows, cols), 0) // r2 % gb
        col_g = lax.broadcasted_iota(jnp.int32, (rows, cols), 1) // c2 % gb
        return jnp.where(row_g == col_g, wide, 0.0).astype(BF16)

    a_grp = (k_ji + skip).transpose(2, 0, 4, 1, 3).reshape(nb, gb, t, SSM_GROUP, t, SSM_GROUP)
    a_op = block_diagonal(a_grp, CHUNK_COLS, CHUNK_COLS)
    in_f = jnp.einsum("jgp,gpi->gjip", powers[t - 1 - idx, 0], b_bar[0])
    in_b = jnp.einsum("jgp,gpi->gjip", powers[idx, 1], b_bar[1])
    bin_grp = jnp.stack([in_f.real, in_f.imag, in_b.real, in_b.imag], axis=3)
    bin_op = block_diagonal(bin_grp.reshape(nb, gb, t, SSM_GROUP, 4, SSM_STATE),
                            CHUNK_COLS, STATE_COLS)
    out_f = jnp.einsum("gop,igp->gpio", c[0], powers[idx + 1, 0])
    out_b = jnp.einsum("gop,igp->gpio", c[1], powers[t - idx, 1])
    cout_grp = jnp.stack([out_f.real, -out_f.imag, out_b.real, -out_b.imag], axis=1)
    cout_op = block_diagonal(cout_grp.reshape(nb, gb, 4, SSM_STATE, t, SSM_GROUP),
                             STATE_COLS, CHUNK_COLS)
    r = jnp.arange(SUBLANES)
    tile_pow = jnp.exp(lam_dt[None] * (t * jnp.arange(SUBLANES + 1, dtype=F32))[:, None, None, None])

    def scan_consts(d, down):
        z = []
        for k in (1, 2, 4):
            keep = (r >= k) if down else (r <= SUBLANES - 1 - k)
            z.append(jnp.where(keep[:, None, None], tile_pow[k, d][None], 0.0))
        z.append(tile_pow[r if down else SUBLANES - 1 - r, d])
        z.append(jnp.broadcast_to(tile_pow[SUBLANES, d][None], z[0].shape))
        return jnp.stack(z)

    zf, zb = scan_consts(0, True), scan_consts(1, False)
    dec = jnp.stack([zf.real, zf.imag, zb.real, zb.imag], axis=2)
    dec = dec.reshape(5, SUBLANES, 4, nb, gb * SSM_STATE).transpose(3, 0, 1, 2, 4)
    dec = dec.reshape(nb, 5, SUBLANES, STATE_COLS)
    return a_op, bin_op, cout_op, dec


def _mix_kernel(x_ref, attn_ref, y_ref, ga_ref, gs_ref, wp_ref, wa_ref, wb_ref, wo_ref, o_ref):
    attn_out = jnp.dot(attn_ref[...], wp_ref[...], preferred_element_type=F32)
    yg = jax.nn.gelu(y_ref[...]).astype(BF16)
    ssm_out = (jnp.dot(yg, wa_ref[...], preferred_element_type=F32)
               * jax.nn.sigmoid(jnp.dot(yg, wb_ref[...], preferred_element_type=F32)))
    merged = ga_ref[...].astype(F32) * attn_out + gs_ref[...].astype(F32) * ssm_out
    o_ref[...] = x_ref[...] + jnp.dot(merged.astype(BF16), wo_ref[...],
                                      preferred_element_type=F32)


def _mix(x2, attn, y, ga, gs, lw, tm):
    n = x2.shape[0]
    row = lambda i: (i, 0)
    full = lambda a: pl.BlockSpec(a.shape, lambda i: (0, 0))
    return pl.pallas_call(
        _mix_kernel, grid=(n // tm,),
        in_specs=[pl.BlockSpec((tm, D_MODEL), row), pl.BlockSpec((tm, Q_WIDTH), row),
                  pl.BlockSpec((tm, SSM_WIDTH), row), pl.BlockSpec((tm, D_MODEL), row),
                  pl.BlockSpec((tm, D_MODEL), row), full(lw["wp"]), full(lw["wa"]),
                  full(lw["wb"]), full(lw["wo"])],
        out_specs=pl.BlockSpec((tm, D_MODEL), row),
        out_shape=jax.ShapeDtypeStruct((n, D_MODEL), F32),
        compiler_params=_cparams("parallel"), name="mix",
    )(x2, attn, y, ga, gs, lw["wp"], lw["wa"], lw["wb"], lw["wo"])


def _mlp_kernel(x_ref, gain_ref, w1_ref, w2_ref, o_ref):
    x = x_ref[...]
    h = x * lax.rsqrt(jnp.mean(x * x, axis=-1, keepdims=True) + EPS) * gain_ref[...]
    a = jnp.dot(h.astype(BF16), w1_ref[...], preferred_element_type=F32)
    a = jnp.square(jnp.maximum(a, 0.0)).astype(BF16)
    o_ref[...] = x + jnp.dot(a, w2_ref[...], preferred_element_type=F32)


def _mlp(x2, lw, tm):
    n = x2.shape[0]
    row = lambda i: (i, 0)
    full = lambda a: pl.BlockSpec(a.shape, lambda i: (0, 0))
    return pl.pallas_call(
        _mlp_kernel, grid=(n // tm,),
        in_specs=[pl.BlockSpec((tm, D_MODEL), row), full(lw["norm_mlp"]), full(lw["w1"]),
                  full(lw["w2"])],
        out_specs=pl.BlockSpec((tm, D_MODEL), row),
        out_shape=jax.ShapeDtypeStruct((n, D_MODEL), F32),
        compiler_params=_cparams("parallel"), name="mlp",
    )(x2, lw["norm_mlp"], lw["w1"], lw["w2"])


def _q_permutation():
    perm = []
    for j in range(KV_PAIRS):
        for m in range(Q_PER_KV):
            for half in range(2):
                head = Q_PER_KV * (2 * j + half) + m
                perm.extend(range(head * HEAD_DIM, (head + 1) * HEAD_DIM))
    return np.asarray(perm, dtype=np.int32)


def _rope_tables(seq_len):
    t = jnp.arange(seq_len)
    inv_freq = ROPE_THETA ** (-jnp.arange(0, AXIS_DIM, 2, dtype=F32) / AXIS_DIM)
    ang = jnp.concatenate([(t // GRID_W).astype(F32)[:, None] * inv_freq[None, :],
                           (t % GRID_W).astype(F32)[:, None] * inv_freq[None, :]], axis=-1)
    cos = jnp.repeat(jnp.cos(ang), 2, axis=-1)
    sin = jnp.repeat(jnp.sin(ang), 2, axis=-1)
    even = (jnp.arange(HEAD_DIM) % 2 == 0)[None, :]
    sa = jnp.where(even, -sin, 0.0)
    sb = jnp.where(even, 0.0, sin)
    two = lambda a: jnp.concatenate([a, a], axis=-1)
    return two(cos), two(sa), two(sb)


def _scores_bounded(q_gain, k_gain):
    bound = (1.05 * HEAD_DIM * Q_SCALE * jnp.max(jnp.abs(q_gain.astype(F32)))
             * jnp.max(jnp.abs(k_gain.astype(F32))))
    return (bound <= MAX_UNSHIFTED_SCORE).astype(jnp.int32).reshape(1)


def _layer_weights(i, p):
    perm = _q_permutation()
    w_in = p["w_in"][i]
    o = 0
    wq = w_in[:, o:o + Q_WIDTH]; o += Q_WIDTH
    wk = w_in[:, o:o + KV_WIDTH]; o += KV_WIDTH
    wv = w_in[:, o:o + KV_WIDTH]; o += KV_WIDTH
    wu = w_in[:, o:o + SSM_WIDTH]; o += SSM_WIDTH
    wga = w_in[:, o:o + D_MODEL]; o += D_MODEL
    wgs = w_in[:, o:o + D_MODEL]
    head_avg = jnp.kron(jnp.eye(LANES // HEAD_DIM, dtype=F32),
                        jnp.full((HEAD_DIM, HEAD_DIM), 1.0 / HEAD_DIM, F32))
    two = lambda a: jnp.concatenate([a, a])[None, :].astype(F32)
    a_op, bin_op, cout_op, dec = _s5_operators(
        p["ssm_lambda_re"][i], p["ssm_lambda_im"][i], p["ssm_log_step"][i], p["ssm_b_re"][i],
        p["ssm_b_im"][i], p["ssm_c_re"][i], p["ssm_c_im"][i], p["ssm_d"][i])
    return {
        "norm_mix": p["norm_mix"][i][None, :].astype(F32),
        "wq": wq[:, perm].astype(BF16), "wk": wk.astype(BF16), "wvt": wv.T.astype(BF16),
        "wu": wu.astype(BF16), "wga": wga.astype(BF16), "wgs": wgs.astype(BF16),
        "qn": two(p["q_norm"][i]), "kn": two(p["k_norm"][i]), "bd": head_avg.astype(BF16),
        "bounded": _scores_bounded(p["q_norm"][i], p["k_norm"][i]),
        "s5_a": a_op, "s5_bin": bin_op, "s5_cout": cout_op, "s5_dec": dec,
        "wp": p["w_attn_proj"][i][perm, :].astype(BF16),
        "wa": p["w_glu_a"][i].astype(BF16), "wb": p["w_glu_b"][i].astype(BF16),
        "wo": p["w_out"][i].astype(BF16),
        "norm_mlp": p["norm_mlp"][i][None, :].astype(F32),
        "w1": p["w_ff1"][i].astype(BF16), "w2": p["w_ff2"][i].astype(BF16),
    }


def _tile(n, pref):
    t = min(n, pref)
    assert n % t == 0, (n, t)
    return t


def _trunk(x, layers):
    batch, seq_len, _ = x.shape
    assert seq_len % (CHUNK * SUBLANES) == 0 and seq_len % LANES == 0
    n = batch * seq_len
    rope = _rope_tables(seq_len)
    tm = _tile(seq_len, 512)
    tq = _tile(seq_len, 512)
    tk = _tile(seq_len, 4096)
    x2 = x.reshape(n, D_MODEL)
    for lw in layers:
        q, k, vt, u, ga, gs = _in_proj(x2, lw, rope, seq_len, tm)
        attn = _attention(lw["bounded"], q, k, vt, batch, seq_len, tq, tk)
        y = _s5(u, lw, batch, seq_len)
        x2 = _mix(x2, attn, y, ga, gs, lw, tm)
        x2 = _mlp(x2, lw, _tile(seq_len, 256))
    return x2.reshape(batch, seq_len, D_MODEL)


def kernel(x_prompt, x_sample, norm_mix, w_in, q_norm, k_norm, w_attn_proj, ssm_lambda_re, ssm_lambda_im, ssm_log_step, ssm_b_re, ssm_b_im, ssm_c_re, ssm_c_im, ssm_d, w_glu_a, w_glu_b, w_out, norm_mlp, w_ff1, w_ff2):
    params = dict(norm_mix=norm_mix, w_in=w_in, q_norm=q_norm, k_norm=k_norm,
                  w_attn_proj=w_attn_proj, ssm_lambda_re=ssm_lambda_re,
                  ssm_lambda_im=ssm_lambda_im, ssm_log_step=ssm_log_step, ssm_b_re=ssm_b_re,
                  ssm_b_im=ssm_b_im, ssm_c_re=ssm_c_re, ssm_c_im=ssm_c_im, ssm_d=ssm_d,
                  w_glu_a=w_glu_a, w_glu_b=w_glu_b, w_out=w_out, norm_mlp=norm_mlp,
                  w_ff1=w_ff1, w_ff2=w_ff2)
    layers = [_layer_weights(i, params) for i in range(norm_mix.shape[0])]
    return (_trunk(x_prompt, layers), _trunk(x_sample, layers))
```

```python
import functools
import math

import jax
import jax.numpy as jnp
import numpy as np
from jax import lax
from jax.experimental import pallas as pl
from jax.experimental.pallas import tpu as pltpu

D_MODEL = 1024
GRID_W = 64
N_Q_HEADS = 16
N_KV_HEADS = 4
HEAD_DIM = 64
ROPE_THETA = 10000.0
AXIS_DIM = HEAD_DIM // 2
Q_WIDTH = N_Q_HEADS * HEAD_DIM
KV_WIDTH = N_KV_HEADS * HEAD_DIM
SSM_WIDTH = 512
SSM_GROUP = 16
N_SSM_GROUPS = SSM_WIDTH // SSM_GROUP
SSM_STATE = 64
D_FF = 4 * D_MODEL
EPS = 1e-6

LANES = 128
SUBLANES = 8
CHUNK = 8
GROUPS_PER_BLOCK = LANES // SSM_GROUP
N_CHANNEL_BLOCKS = SSM_WIDTH // LANES
CHUNK_COLS = CHUNK * LANES
STATE_PART = GROUPS_PER_BLOCK * SSM_STATE
STATE_COLS = 4 * STATE_PART
S5_ROW_BLOCK = 256
KV_PAIRS = N_KV_HEADS // 2
Q_PER_KV = N_Q_HEADS // N_KV_HEADS
VT_ROWS = KV_PAIRS * 4 * HEAD_DIM
LOG2E = math.log2(math.e)
Q_SCALE = HEAD_DIM ** -0.5 * LOG2E
MAX_UNSHIFTED_SCORE = 100.0
SHIFTED_KEY_TILE = 512
VMEM_LIMIT = 56 * 1024 * 1024

BF16 = jnp.bfloat16
F32 = jnp.float32


def _cparams(*sem):
    return pltpu.CompilerParams(dimension_semantics=sem, vmem_limit_bytes=VMEM_LIMIT)


def _in_proj_kernel(x_ref, gain_ref, wq_ref, wk_ref, wvt_ref, wu_ref, wga_ref, wgs_ref,
                    qn_ref, kn_ref, bd_ref, cos_ref, sa_ref, sb_ref,
                    q_ref, k_ref, vt_ref, u_ref, ga_ref, gs_ref):
    x = x_ref[...]
    xn = x * lax.rsqrt(jnp.mean(x * x, axis=-1, keepdims=True) + EPS) * gain_ref[...]
    xn = xn.astype(BF16)
    cos = cos_ref[...]
    sa = sa_ref[...]
    sb = sb_ref[...]
    bd = bd_ref[...]

    def head_norm_rope(blk, gain, scale):
        msq = jnp.dot((blk * blk).astype(BF16), bd, preferred_element_type=F32)
        y = blk * lax.rsqrt(msq + EPS) * gain
        y = y * cos + pltpu.roll(y, LANES - 1, 1) * sa + pltpu.roll(y, 1, 1) * sb
        return y * scale

    q = jnp.dot(xn, wq_ref[...], preferred_element_type=F32)
    for b in range(Q_WIDTH // LANES):
        sl = slice(b * LANES, (b + 1) * LANES)
        q_ref[:, sl] = head_norm_rope(q[:, sl], qn_ref[...], Q_SCALE).astype(BF16)
    k = jnp.dot(xn, wk_ref[...], preferred_element_type=F32)
    for b in range(KV_WIDTH // LANES):
        sl = slice(b * LANES, (b + 1) * LANES)
        k_ref[:, sl] = head_norm_rope(k[:, sl], kn_ref[...], 1.0).astype(BF16)
    vt = lax.dot_general(wvt_ref[...], xn, (((1,), (1,)), ((), ())), preferred_element_type=F32)
    vt = vt.astype(BF16)
    ones = jnp.ones((2 * HEAD_DIM, vt.shape[1]), BF16)
    for j in range(KV_PAIRS):
        src, dst = 2 * j * HEAD_DIM, 4 * j * HEAD_DIM
        vt_ref[dst:dst + HEAD_DIM, :] = vt[src:src + HEAD_DIM, :]
        vt_ref[dst + HEAD_DIM:dst + 3 * HEAD_DIM, :] = ones
        vt_ref[dst + 3 * HEAD_DIM:dst + 4 * HEAD_DIM, :] = vt[src + HEAD_DIM:src + 2 * HEAD_DIM, :]
    u_ref[...] = jnp.dot(xn, wu_ref[...], preferred_element_type=F32)
    ga = jnp.dot(xn, wga_ref[...], preferred_element_type=F32)
    ga_ref[...] = jax.nn.sigmoid(ga).astype(BF16)
    gs = jnp.dot(xn, wgs_ref[...], preferred_element_type=F32)
    gs_ref[...] = jax.nn.sigmoid(gs).astype(BF16)


def _in_proj(x2, lw, rope, seq_len, tm):
    n = x2.shape[0]
    tiles_per_seq = seq_len // tm
    row = lambda i: (i, 0)
    const = lambda i: (0, 0)
    pos = lambda i: (i % tiles_per_seq, 0)
    full = lambda a: pl.BlockSpec(a.shape, const)
    cos, sa, sb = rope
    in_specs = [
        pl.BlockSpec((tm, D_MODEL), row), full(lw["norm_mix"]),
        full(lw["wq"]), full(lw["wk"]), full(lw["wvt"]), full(lw["wu"]), full(lw["wga"]),
        full(lw["wgs"]), full(lw["qn"]), full(lw["kn"]), full(lw["bd"]),
        pl.BlockSpec((tm, LANES), pos), pl.BlockSpec((tm, LANES), pos),
        pl.BlockSpec((tm, LANES), pos),
    ]
    out_shape = [
        jax.ShapeDtypeStruct((n, Q_WIDTH), BF16), jax.ShapeDtypeStruct((n, KV_WIDTH), BF16),
        jax.ShapeDtypeStruct((VT_ROWS, n), BF16), jax.ShapeDtypeStruct((n, SSM_WIDTH), F32),
        jax.ShapeDtypeStruct((n, D_MODEL), BF16), jax.ShapeDtypeStruct((n, D_MODEL), BF16),
    ]
    out_specs = [
        pl.BlockSpec((tm, Q_WIDTH), row), pl.BlockSpec((tm, KV_WIDTH), row),
        pl.BlockSpec((VT_ROWS, tm), lambda i: (0, i)), pl.BlockSpec((tm, SSM_WIDTH), row),
        pl.BlockSpec((tm, D_MODEL), row), pl.BlockSpec((tm, D_MODEL), row),
    ]
    return pl.pallas_call(
        _in_proj_kernel, grid=(n // tm,), in_specs=in_specs, out_specs=out_specs,
        out_shape=out_shape, compiler_params=_cparams("parallel"), name="in_proj",
    )(x2, lw["norm_mix"], lw["wq"], lw["wk"], lw["wvt"], lw["wu"], lw["wga"], lw["wgs"],
      lw["qn"], lw["kn"], lw["bd"], cos, sa, sb)


def _attention_kernel(bounded_ref, q_ref, k_ref, vt_ref, o_ref, *, tk):
    tq = q_ref.shape[0]
    seq_len = k_ref.shape[0]
    tk_shifted = min(tk, SHIFTED_KEY_TILE)
    lane = lax.broadcasted_iota(jnp.int32, (tq, LANES), 1)
    row = lax.broadcasted_iota(jnp.int32, (LANES, tq), 0)
    nt = (((1,), (1,)), ((), ()))

    def head_pair(m):
        qm = q_ref[:, m * LANES:(m + 1) * LANES]
        return (jnp.where(lane < HEAD_DIM, qm, jnp.zeros_like(qm)),
                jnp.where(lane >= HEAD_DIM, qm, jnp.zeros_like(qm)))

    def tiles(t, width):
        start = pl.multiple_of(t * width, width)
        return (k_ref[pl.ds(start, width), :], vt_ref[0:LANES, pl.ds(start, width)],
                vt_ref[LANES:2 * LANES, pl.ds(start, width)])

    def finish(m, acc_a, acc_b):
        o_t = jnp.where(row < HEAD_DIM, acc_a / acc_a[HEAD_DIM:HEAD_DIM + 1, :],
                        acc_b / acc_b[0:1, :])
        o_ref[:, m * LANES:(m + 1) * LANES] = o_t.T.astype(BF16)

    @pl.when(bounded_ref[0] != 0)
    def _():
        for m in range(Q_PER_KV):
            q_a, q_b = head_pair(m)

            q_ab = jnp.concatenate([q_a, q_b], axis=0)

            def body(t, carry):
                kt, va, vb = tiles(t, tk)
                s = lax.dot_general(kt, q_ab, nt, preferred_element_type=F32)
                p = jnp.exp2(s).astype(BF16)
                return (carry[0] + jnp.dot(va, p[:, :tq], preferred_element_type=F32),
                        carry[1] + jnp.dot(vb, p[:, tq:], preferred_element_type=F32))

            zero = jnp.zeros((LANES, tq), F32)
            acc_a, acc_b = lax.fori_loop(0, seq_len // tk, body, (zero, zero), unroll=True)
            finish(m, acc_a, acc_b)

    @pl.when(bounded_ref[0] == 0)
    def _():
        for m in range(Q_PER_KV):
            q_a, q_b = head_pair(m)

            def body(t, carry):
                kt, va, vb = tiles(t, tk_shifted)
                new = []
                for qh, vh, (mx, acc) in zip((q_a, q_b), (va, vb), carry):
                    s = lax.dot_general(kt, qh, nt, preferred_element_type=F32)
                    mx_new = jnp.maximum(mx, jnp.max(s, axis=0, keepdims=True))
                    p = jnp.exp2(s - mx_new).astype(BF16)
                    acc_new = (jnp.exp2(mx - mx_new) * acc
                               + jnp.dot(vh, p, preferred_element_type=F32))
                    new.append((mx_new, acc_new))
                return tuple(new)

            init = tuple((jnp.full((1, tq), -jnp.inf, F32), jnp.zeros((LANES, tq), F32))
                         for _ in range(2))
            (_, acc_a), (_, acc_b) = lax.fori_loop(0, seq_len // tk_shifted, body, init)
            finish(m, acc_a, acc_b)


def _attention(bounded, q, k, vt, batch, seq_len, tq, tk):
    n = q.shape[0]
    q_tiles = seq_len // tq
    pair_w = Q_PER_KV * LANES
    qmap = lambda b, j, i: (b * q_tiles + i, j)
    return pl.pallas_call(
        functools.partial(_attention_kernel, tk=tk),
        grid=(batch, KV_PAIRS, q_tiles),
        in_specs=[pl.BlockSpec(memory_space=pltpu.SMEM),
                  pl.BlockSpec((tq, pair_w), qmap),
                  pl.BlockSpec((seq_len, LANES), lambda b, j, i: (b, j)),
                  pl.BlockSpec((2 * LANES, seq_len), lambda b, j, i: (j, b))],
        out_specs=pl.BlockSpec((tq, pair_w), qmap),
        out_shape=jax.ShapeDtypeStruct((n, Q_WIDTH), BF16),
        compiler_params=_cparams("parallel", "parallel", "parallel"), name="attention",
    )(bounded, q, k, vt)


def _cmul(ar, ai, xr, xi):
    return ar * xr - ai * xi, ar * xi + ai * xr


def _s5_kernel(u_ref, a_ref, bin_ref, cout_ref, dec_ref, y_ref, x_ref, s_ref, ya_ref):
    n_chunks = x_ref.shape[0]
    n_tiles = n_chunks // SUBLANES
    row_block = min(S5_ROW_BLOCK, n_chunks)
    for i in range(CHUNK):
        x_ref[:, i * LANES:(i + 1) * LANES] = (
            u_ref[pl.ds(i, n_chunks, stride=CHUNK), :].astype(BF16))
    for r in range(n_chunks // row_block):
        rs = slice(r * row_block, (r + 1) * row_block)
        s_ref[rs, :] = jnp.dot(x_ref[rs, :], bin_ref[0], preferred_element_type=F32)
    f_re, f_im = slice(0, STATE_PART), slice(STATE_PART, 2 * STATE_PART)
    b_re, b_im = slice(2 * STATE_PART, 3 * STATE_PART), slice(3 * STATE_PART, 4 * STATE_PART)
    rows = lax.broadcasted_iota(jnp.int32, (SUBLANES, STATE_PART), 0)

    def scan_tile(xr, xi, cr, ci, re, im, down):
        for k in range(3):
            shift = (1 << k) if down else SUBLANES - (1 << k)
            pr, pi = _cmul(dec_ref[0, k, :, re], dec_ref[0, k, :, im],
                           pltpu.roll(xr, shift, 0), pltpu.roll(xi, shift, 0))
            xr, xi = xr + pr, xi + pi
        one = 1 if down else SUBLANES - 1
        edge = 0 if down else SUBLANES - 1
        er, ei = _cmul(dec_ref[0, 3, :, re], dec_ref[0, 3, :, im], cr, ci)
        er = er + jnp.where(rows == edge, 0.0, pltpu.roll(xr, one, 0))
        ei = ei + jnp.where(rows == edge, 0.0, pltpu.roll(xi, one, 0))
        last = SUBLANES - 1 - edge
        nr, ni = _cmul(dec_ref[0, 4, 0:1, re], dec_ref[0, 4, 0:1, im], cr, ci)
        return er, ei, nr + xr[last:last + 1, :], ni + xi[last:last + 1, :]

    for r in range(n_chunks // row_block):
        rs = slice(r * row_block, (r + 1) * row_block)
        ya_ref[rs, :] = jnp.dot(x_ref[rs, :], a_ref[0], preferred_element_type=F32)
    hr = hi = gr = gi = jnp.zeros((1, STATE_PART), F32)
    for t in range(n_tiles):
        rf = slice(t * SUBLANES, (t + 1) * SUBLANES)
        rb = slice((n_tiles - 1 - t) * SUBLANES, (n_tiles - t) * SUBLANES)
        er, ei, hr, hi = scan_tile(s_ref[rf, f_re], s_ref[rf, f_im], hr, hi, f_re, f_im, True)
        s_ref[rf, f_re] = er
        s_ref[rf, f_im] = ei
        er, ei, gr, gi = scan_tile(s_ref[rb, b_re], s_ref[rb, b_im], gr, gi, b_re, b_im, False)
        s_ref[rb, b_re] = er
        s_ref[rb, b_im] = ei
    for r in range(n_chunks // row_block):
        rs = slice(r * row_block, (r + 1) * row_block)
        y = ya_ref[rs, :] + jnp.dot(s_ref[rs, :].astype(BF16), cout_ref[0],
                                    preferred_element_type=F32)
        for i in range(CHUNK):
            y_ref[pl.ds(r * row_block * CHUNK + i, row_block, stride=CHUNK), :] = (
                y[:, i * LANES:(i + 1) * LANES])


def _s5(u, lw, batch, seq_len):
    n_chunks = seq_len // CHUNK
    w3 = lambda b, s: (b, 0, 0)
    io = pl.BlockSpec((seq_len, LANES), lambda b, s: (s, b))
    return pl.pallas_call(
        _s5_kernel, grid=(N_CHANNEL_BLOCKS, batch),
        in_specs=[io,
                  pl.BlockSpec((1, CHUNK_COLS, CHUNK_COLS), w3),
                  pl.BlockSpec((1, CHUNK_COLS, STATE_COLS), w3),
                  pl.BlockSpec((1, STATE_COLS, CHUNK_COLS), w3),
                  pl.BlockSpec((1, 5, SUBLANES, STATE_COLS), lambda b, s: (b, 0, 0, 0))],
        out_specs=io,
        out_shape=jax.ShapeDtypeStruct(u.shape, F32),
        scratch_shapes=[pltpu.VMEM((n_chunks, CHUNK_COLS), BF16),
                        pltpu.VMEM((n_chunks, STATE_COLS), F32),
                        pltpu.VMEM((n_chunks, CHUNK_COLS), F32)],
        compiler_params=_cparams("parallel", "parallel"), name="s5",
    )(u, lw["s5_a"], lw["s5_bin"], lw["s5_cout"], lw["s5_dec"])


def _s5_operators(lam_re, lam_im, log_step, b_re, b_im, c_re, c_im, d_skip):
    t = CHUNK
    lam = lax.complex(lam_re.astype(F32), lam_im.astype(F32))
    step = jnp.exp(log_step.astype(F32))
    lam_dt = lam * step[..., None]
    lam_bar = jnp.exp(lam_dt)
    b_bar = ((lam_bar - 1.0) / lam)[..., None] * lax.complex(b_re.astype(F32), b_im.astype(F32))
    c = lax.complex(c_re.astype(F32), c_im.astype(F32))
    powers = jnp.exp(lam_dt[None] * jnp.arange(t + 1, dtype=F32)[:, None, None, None])
    taps = jnp.einsum("dgop,ndgp,dgpi->ndgoi", c, powers[:t], b_bar).real
    idx = jnp.arange(t)
    lag = idx[None, :] - idx[:, None]
    fwd = jnp.where((lag >= 0)[:, :, None, None, None], taps[jnp.clip(lag, 0, t - 1), 0], 0.0)
    bwd = jnp.where((lag <= 0)[:, :, None, None, None], taps[jnp.clip(-lag, 0, t - 1), 1], 0.0)
    k_ji = fwd + bwd
    eye_t = jnp.eye(t, dtype=F32)
    eye_c = jnp.eye(SSM_GROUP, dtype=F32)
    skip = (eye_t[:, :, None, None, None] * eye_c[None, None, None]
            * d_skip.astype(F32).reshape(N_SSM_GROUPS, SSM_GROUP)[None, None, :, :, None])
    nb, gb = N_CHANNEL_BLOCKS, GROUPS_PER_BLOCK

    def block_diagonal(z, rows, cols):
        r2, c1, c2 = z.shape[3], z.shape[4], z.shape[5]
        compact = z.astype(BF16).transpose(0, 2, 1, 3, 4, 5).reshape(nb, rows, c1 * c2)
        src = np.arange(cols) // (gb * c2) * c2 + np.arange(cols) % c2
        spread = jnp.asarray(np.arange(c1 * c2)[:, None] == src[None, :], BF16)
        wide = jnp.einsum("brk,kc->brc", compact, spread, preferred_element_type=F32)
        row_g = lax.broadcasted_iota(jnp.int32, (rows, cols), 0) // r2 % gb
        col_g = lax.broadcasted_iota(jnp.int32, (rows, cols), 1) // c2 % gb
        return jnp.where(row_g == col_g, wide, 0.0).astype(BF16)

    a_grp = (k_ji + skip).transpose(2, 0, 4, 1, 3).reshape(nb, gb, t, SSM_GROUP, t, SSM_GROUP)
    a_op = block_diagonal(a_grp, CHUNK_COLS, CHUNK_COLS)
    in_f = jnp.einsum("jgp,gpi->gjip", powers[t - 1 - idx, 0], b_bar[0])
    in_b = jnp.einsum("jgp,gpi->gjip", powers[idx, 1], b_bar[1])
    bin_grp = jnp.stack([in_f.real, in_f.imag, in_b.real, in_b.imag], axis=3)
    bin_op = block_diagonal(bin_grp.reshape(nb, gb, t, SSM_GROUP, 4, SSM_STATE),
                            CHUNK_COLS, STATE_COLS)
    out_f = jnp.einsum("gop,igp->gpio", c[0], powers[idx + 1, 0])
    out_b = jnp.einsum("gop,igp->gpio", c[1], powers[t - idx, 1])
    cout_grp = jnp.stack([out_f.real, -out_f.imag, out_b.real, -out_b.imag], axis=1)
    cout_op = block_diagonal(cout_grp.reshape(nb, gb, 4, SSM_STATE, t, SSM_GROUP),
                             STATE_COLS, CHUNK_COLS)
    r = jnp.arange(SUBLANES)
    tile_pow = jnp.exp(lam_dt[None] * (t * jnp.arange(SUBLANES + 1, dtype=F32))[:, None, None, None])

    def scan_consts(d, down):
        z = []
        for k in (1, 2, 4):
            keep = (r >= k) if down else (r <= SUBLANES - 1 - k)
            z.append(jnp.where(keep[:, None, None], tile_pow[k, d][None], 0.0))
        z.append(tile_pow[r if down else SUBLANES - 1 - r, d])
        z.append(jnp.broadcast_to(tile_pow[SUBLANES, d][None], z[0].shape))
        return jnp.stack(z)

    zf, zb = scan_consts(0, True), scan_consts(1, False)
    dec = jnp.stack([zf.real, zf.imag, zb.real, zb.imag], axis=2)
    dec = dec.reshape(5, SUBLANES, 4, nb, gb * SSM_STATE).transpose(3, 0, 1, 2, 4)
    dec = dec.reshape(nb, 5, SUBLANES, STATE_COLS)
    return a_op, bin_op, cout_op, dec


def _mix_kernel(x_ref, attn_ref, y_ref, ga_ref, gs_ref, wp_ref, wa_ref, wb_ref, wo_ref, o_ref):
    attn_out = jnp.dot(attn_ref[...], wp_ref[...], preferred_element_type=F32)
    yg = jax.nn.gelu(y_ref[...]).astype(BF16)
    ssm_out = (jnp.dot(yg, wa_ref[...], preferred_element_type=F32)
               * jax.nn.sigmoid(jnp.dot(yg, wb_ref[...], preferred_element_type=F32)))
    merged = ga_ref[...].astype(F32) * attn_out + gs_ref[...].astype(F32) * ssm_out
    o_ref[...] = x_ref[...] + jnp.dot(merged.astype(BF16), wo_ref[...],
                                      preferred_element_type=F32)


def _mix(x2, attn, y, ga, gs, lw, tm):
    n = x2.shape[0]
    row = lambda i: (i, 0)
    full = lambda a: pl.BlockSpec(a.shape, lambda i: (0, 0))
    return pl.pallas_call(
        _mix_kernel, grid=(n // tm,),
        in_specs=[pl.BlockSpec((tm, D_MODEL), row), pl.BlockSpec((tm, Q_WIDTH), row),
                  pl.BlockSpec((tm, SSM_WIDTH), row), pl.BlockSpec((tm, D_MODEL), row),
                  pl.BlockSpec((tm, D_MODEL), row), full(lw["wp"]), full(lw["wa"]),
                  full(lw["wb"]), full(lw["wo"])],
        out_specs=pl.BlockSpec((tm, D_MODEL), row),
        out_shape=jax.ShapeDtypeStruct((n, D_MODEL), F32),
        compiler_params=_cparams("parallel"), name="mix",
    )(x2, attn, y, ga, gs, lw["wp"], lw["wa"], lw["wb"], lw["wo"])


def _mlp_kernel(x_ref, gain_ref, w1_ref, w2_ref, o_ref):
    x = x_ref[...]
    h = x * lax.rsqrt(jnp.mean(x * x, axis=-1, keepdims=True) + EPS) * gain_ref[...]
    a = jnp.dot(h.astype(BF16), w1_ref[...], preferred_element_type=F32)
    a = jnp.square(jnp.maximum(a, 0.0)).astype(BF16)
    o_ref[...] = x + jnp.dot(a, w2_ref[...], preferred_element_type=F32)


def _mlp(x2, lw, tm):
    n = x2.shape[0]
    row = lambda i: (i, 0)
    full = lambda a: pl.BlockSpec(a.shape, lambda i: (0, 0))
    return pl.pallas_call(
        _mlp_kernel, grid=(n // tm,),
        in_specs=[pl.BlockSpec((tm, D_MODEL), row), full(lw["norm_mlp"]), full(lw["w1"]),
                  full(lw["w2"])],
        out_specs=pl.BlockSpec((tm, D_MODEL), row),
        out_shape=jax.ShapeDtypeStruct((n, D_MODEL), F32),
        compiler_params=_cparams("parallel"), name="mlp",
    )(x2, lw["norm_mlp"], lw["w1"], lw["w2"])


def _q_permutation():
    perm = []
    for j in range(KV_PAIRS):
        for m in range(Q_PER_KV):
            for half in range(2):
                head = Q_PER_KV * (2 * j + half) + m
                perm.extend(range(head * HEAD_DIM, (head + 1) * HEAD_DIM))
    return np.asarray(perm, dtype=np.int32)


def _rope_tables(seq_len):
    t = jnp.arange(seq_len)
    inv_freq = ROPE_THETA ** (-jnp.arange(0, AXIS_DIM, 2, dtype=F32) / AXIS_DIM)
    ang = jnp.concatenate([(t // GRID_W).astype(F32)[:, None] * inv_freq[None, :],
                           (t % GRID_W).astype(F32)[:, None] * inv_freq[None, :]], axis=-1)
    cos = jnp.repeat(jnp.cos(ang), 2, axis=-1)
    sin = jnp.repeat(jnp.sin(ang), 2, axis=-1)
    even = (jnp.arange(HEAD_DIM) % 2 == 0)[None, :]
    sa = jnp.where(even, -sin, 0.0)
    sb = jnp.where(even, 0.0, sin)
    two = lambda a: jnp.concatenate([a, a], axis=-1)
    return two(cos), two(sa), two(sb)


def _scores_bounded(q_gain, k_gain):
    bound = (1.05 * HEAD_DIM * Q_SCALE * jnp.max(jnp.abs(q_gain.astype(F32)))
             * jnp.max(jnp.abs(k_gain.astype(F32))))
    return (bound <= MAX_UNSHIFTED_SCORE).astype(jnp.int32).reshape(1)


def _layer_weights(i, p):
    perm = _q_permutation()
    w_in = p["w_in"][i]
    o = 0
    wq = w_in[:, o:o + Q_WIDTH]; o += Q_WIDTH
    wk = w_in[:, o:o + KV_WIDTH]; o += KV_WIDTH
    wv = w_in[:, o:o + KV_WIDTH]; o += KV_WIDTH
    wu = w_in[:, o:o + SSM_WIDTH]; o += SSM_WIDTH
    wga = w_in[:, o:o + D_MODEL]; o += D_MODEL
    wgs = w_in[:, o:o + D_MODEL]
    head_avg = jnp.kron(jnp.eye(LANES // HEAD_DIM, dtype=F32),
                        jnp.full((HEAD_DIM, HEAD_DIM), 1.0 / HEAD_DIM, F32))
    two = lambda a: jnp.concatenate([a, a])[None, :].astype(F32)
    a_op, bin_op, cout_op, dec = (z[i] for z in p["s5_ops"])
    return {
        "norm_mix": p["norm_mix"][i][None, :].astype(F32),
        "wq": wq[:, perm].astype(BF16), "wk": wk.astype(BF16), "wvt": wv.T.astype(BF16),
        "wu": wu.astype(BF16), "wga": wga.astype(BF16), "wgs": wgs.astype(BF16),
        "qn": two(p["q_norm"][i]), "kn": two(p["k_norm"][i]), "bd": head_avg.astype(BF16),
        "bounded": _scores_bounded(p["q_norm"][i], p["k_norm"][i]),
        "s5_a": a_op, "s5_bin": bin_op, "s5_cout": cout_op, "s5_dec": dec,
        "wp": p["w_attn_proj"][i][perm, :].astype(BF16),
        "wa": p["w_glu_a"][i].astype(BF16), "wb": p["w_glu_b"][i].astype(BF16),
        "wo": p["w_out"][i].astype(BF16),
        "norm_mlp": p["norm_mlp"][i][None, :].astype(F32),
        "w1": p["w_ff1"][i].astype(BF16), "w2": p["w_ff2"][i].astype(BF16),
    }


def _tiles(seq_len):
    assert seq_len % (CHUNK * SUBLANES) == 0 and seq_len % LANES == 0
    prefs = dict(proj_rows=512, mlp_rows=256, queries=512, keys=4096)
    tiles = {name: min(seq_len, pref) for name, pref in prefs.items()}
    assert all(seq_len % t == 0 for t in tiles.values()), (seq_len, tiles)
    return tiles


def _trunk(x, layers):
    batch, seq_len, _ = x.shape
    n = batch * seq_len
    rope = _rope_tables(seq_len)
    t = _tiles(seq_len)
    x2 = x.reshape(n, D_MODEL)
    for lw in layers:
        q, k, vt, u, ga, gs = _in_proj(x2, lw, rope, seq_len, t["proj_rows"])
        attn = _attention(lw["bounded"], q, k, vt, batch, seq_len, t["queries"], t["keys"])
        y = _s5(u, lw, batch, seq_len)
        x2 = _mix(x2, attn, y, ga, gs, lw, t["proj_rows"])
        x2 = _mlp(x2, lw, t["mlp_rows"])
    return x2.reshape(batch, seq_len, D_MODEL)


def kernel(x_prompt, x_sample, norm_mix, w_in, q_norm, k_norm, w_attn_proj, ssm_lambda_re, ssm_lambda_im, ssm_log_step, ssm_b_re, ssm_b_im, ssm_c_re, ssm_c_im, ssm_d, w_glu_a, w_glu_b, w_out, norm_mlp, w_ff1, w_ff2):
    params = dict(norm_mix=norm_mix, w_in=w_in, q_norm=q_norm, k_norm=k_norm,
                  w_attn_proj=w_attn_proj, ssm_lambda_re=ssm_lambda_re,
                  ssm_lambda_im=ssm_lambda_im, ssm_log_step=ssm_log_step, ssm_b_re=ssm_b_re,
                  ssm_b_im=ssm_b_im, ssm_c_re=ssm_c_re, ssm_c_im=ssm_c_im, ssm_d=ssm_d,
                  w_glu_a=w_glu_a, w_glu_b=w_glu_b, w_out=w_out, norm_mlp=norm_mlp,
                  w_ff1=w_ff1, w_ff2=w_ff2)
    params["s5_ops"] = jax.vmap(_s5_operators)(
        ssm_lambda_re, ssm_lambda_im, ssm_log_step, ssm_b_re, ssm_b_im, ssm_c_re, ssm_c_im, ssm_d)
    layers =[_layer_weights(i, params) for i in range(norm_mix.shape[0])]
    return (_trunk(x_prompt, layers), _trunk(x_sample, layers))
```

```python
import functools
import math

import jax
import jax.numpy as jnp
import numpy as np
from jax import lax
from jax.experimental import pallas as pl
from jax.experimental.pallas import tpu as pltpu

D_MODEL = 1024
GRID_W = 64
N_Q_HEADS = 16
N_KV_HEADS = 4
HEAD_DIM = 64
ROPE_THETA = 10000.0
AXIS_DIM = HEAD_DIM // 2
Q_WIDTH = N_Q_HEADS * HEAD_DIM
KV_WIDTH = N_KV_HEADS * HEAD_DIM
SSM_WIDTH = 512
SSM_GROUP = 16
N_SSM_GROUPS = SSM_WIDTH // SSM_GROUP
SSM_STATE = 64
D_FF = 4 * D_MODEL
EPS = 1e-6

LANES = 128
SUBLANES = 8
CHUNK = 8
GROUPS_PER_BLOCK = LANES // SSM_GROUP
N_CHANNEL_BLOCKS = SSM_WIDTH // LANES
CHUNK_COLS = CHUNK * LANES
STATE_PART = GROUPS_PER_BLOCK * SSM_STATE
STATE_COLS = 4 * STATE_PART
S5_ROW_BLOCK = 256
KV_PAIRS = N_KV_HEADS // 2
Q_PER_KV = N_Q_HEADS // N_KV_HEADS
VT_ROWS = KV_PAIRS * 4 * HEAD_DIM
LOG2E = math.log2(math.e)
Q_SCALE = HEAD_DIM ** -0.5 * LOG2E
MAX_UNSHIFTED_SCORE = 100.0
SHIFTED_KEY_TILE = 512
VMEM_LIMIT = 56 * 1024 * 1024

BF16 = jnp.bfloat16
F32 = jnp.float32


def _cparams(*sem):
    return pltpu.CompilerParams(dimension_semantics=sem, vmem_limit_bytes=VMEM_LIMIT)


def _in_proj_kernel(x_ref, gain_ref, wq_ref, wk_ref, wvt_ref, wu_ref, wga_ref, wgs_ref,
                    qn_ref, kn_ref, cos_ref, sa_ref, sb_ref,
                    q_ref, k_ref, vt_ref, u_ref, ga_ref, gs_ref):
    x = x_ref[...]
    xn = x * lax.rsqrt(jnp.mean(x * x, axis=-1, keepdims=True) + EPS) * gain_ref[...]
    xn = xn.astype(BF16)
    cos = cos_ref[...]
    sa = sa_ref[...]
    sb = sb_ref[...]

    left_head = lax.broadcasted_iota(jnp.int32, (x.shape[0], LANES), 1) < HEAD_DIM

    def head_norm_rope(blk, gain, scale):
        sq = blk * blk
        both = jnp.sum(sq, axis=-1, keepdims=True)
        left = jnp.sum(jnp.where(left_head, sq, 0.0), axis=-1, keepdims=True)
        msq = jnp.where(left_head, left, both - left) * (1.0 / HEAD_DIM)
        y = blk * lax.rsqrt(msq + EPS) * gain
        y = y * cos + pltpu.roll(y, LANES - 1, 1) * sa + pltpu.roll(y, 1, 1) * sb
        return y * scale

    q = jnp.dot(xn, wq_ref[...], preferred_element_type=F32)
    for b in range(Q_WIDTH // LANES):
        sl = slice(b * LANES, (b + 1) * LANES)
        q_ref[:, sl] = head_norm_rope(q[:, sl], qn_ref[...], Q_SCALE).astype(BF16)
    k = jnp.dot(xn, wk_ref[...], preferred_element_type=F32)
    for b in range(KV_WIDTH // LANES):
        sl = slice(b * LANES, (b + 1) * LANES)
        k_ref[:, sl] = head_norm_rope(k[:, sl], kn_ref[...], 1.0).astype(BF16)
    vt = lax.dot_general(wvt_ref[...], xn, (((1,), (1,)), ((), ())), preferred_element_type=F32)
    vt = vt.astype(BF16)
    ones = jnp.ones((2 * HEAD_DIM, vt.shape[1]), BF16)
    for j in range(KV_PAIRS):
        src, dst = 2 * j * HEAD_DIM, 4 * j * HEAD_DIM
        vt_ref[dst:dst + HEAD_DIM, :] = vt[src:src + HEAD_DIM, :]
        vt_ref[dst + HEAD_DIM:dst + 3 * HEAD_DIM, :] = ones
        vt_ref[dst + 3 * HEAD_DIM:dst + 4 * HEAD_DIM, :] = vt[src + HEAD_DIM:src + 2 * HEAD_DIM, :]
    u_ref[...] = jnp.dot(xn, wu_ref[...], preferred_element_type=F32)
    ga = jnp.dot(xn, wga_ref[...], preferred_element_type=F32)
    ga_ref[...] = jax.nn.sigmoid(ga).astype(BF16)
    gs = jnp.dot(xn, wgs_ref[...], preferred_element_type=F32)
    gs_ref[...] = jax.nn.sigmoid(gs).astype(BF16)


def _in_proj(x2, lw, rope, seq_len, tm):
    n = x2.shape[0]
    tiles_per_seq = seq_len // tm
    row = lambda i: (i, 0)
    const = lambda i: (0, 0)
    pos = lambda i: (i % tiles_per_seq, 0)
    full = lambda a: pl.BlockSpec(a.shape, const)
    cos, sa, sb = rope
    in_specs = [
        pl.BlockSpec((tm, D_MODEL), row), full(lw["norm_mix"]),
        full(lw["wq"]), full(lw["wk"]), full(lw["wvt"]), full(lw["wu"]), full(lw["wga"]),
        full(lw["wgs"]), full(lw["qn"]), full(lw["kn"]),
        pl.BlockSpec((tm, LANES), pos), pl.BlockSpec((tm, LANES), pos),
        pl.BlockSpec((tm, LANES), pos),
    ]
    out_shape = [
        jax.ShapeDtypeStruct((n, Q_WIDTH), BF16), jax.ShapeDtypeStruct((n, KV_WIDTH), BF16),
        jax.ShapeDtypeStruct((VT_ROWS, n), BF16), jax.ShapeDtypeStruct((n, SSM_WIDTH), F32),
        jax.ShapeDtypeStruct((n, D_MODEL), BF16), jax.ShapeDtypeStruct((n, D_MODEL), BF16),
    ]
    out_specs = [
        pl.BlockSpec((tm, Q_WIDTH), row), pl.BlockSpec((tm, KV_WIDTH), row),
        pl.BlockSpec((VT_ROWS, tm), lambda i: (0, i)), pl.BlockSpec((tm, SSM_WIDTH), row),
        pl.BlockSpec((tm, D_MODEL), row), pl.BlockSpec((tm, D_MODEL), row),
    ]
    return pl.pallas_call(
        _in_proj_kernel, grid=(n // tm,), in_specs=in_specs, out_specs=out_specs,
        out_shape=out_shape, compiler_params=_cparams("parallel"), name="in_proj",
    )(x2, lw["norm_mix"], lw["wq"], lw["wk"], lw["wvt"], lw["wu"], lw["wga"], lw["wgs"],
      lw["qn"], lw["kn"], cos, sa, sb)


def _attention_kernel(bounded_ref, q_ref, k_ref, vt_ref, o_ref, *, tk):
    tq = q_ref.shape[0]
    seq_len = k_ref.shape[0]
    tk_shifted = min(tk, SHIFTED_KEY_TILE)
    lane = lax.broadcasted_iota(jnp.int32, (tq, LANES), 1)
    row = lax.broadcasted_iota(jnp.int32, (LANES, tq), 0)
    nt = (((1,), (1,)), ((), ()))

    def head_pair(m):
        qm = q_ref[:, m * LANES:(m + 1) * LANES]
        return (jnp.where(lane < HEAD_DIM, qm, jnp.zeros_like(qm)),
                jnp.where(lane >= HEAD_DIM, qm, jnp.zeros_like(qm)))

    def tiles(t, width):
        start = pl.multiple_of(t * width, width)
        return (k_ref[pl.ds(start, width), :], vt_ref[0:LANES, pl.ds(start, width)],
                vt_ref[LANES:2 * LANES, pl.ds(start, width)])

    def finish(m, acc_a, acc_b):
        o_t = jnp.where(row < HEAD_DIM, acc_a / acc_a[HEAD_DIM:HEAD_DIM + 1, :],
                        acc_b / acc_b[0:1, :])
        o_ref[:, m * LANES:(m + 1) * LANES] = o_t.T.astype(BF16)

    @pl.when(bounded_ref[0] != 0)
    def _():
        for m in range(Q_PER_KV):
            q_a, q_b = head_pair(m)

            q_ab = jnp.concatenate([q_a, q_b], axis=0)

            def body(t, carry):
                kt, va, vb = tiles(t, tk)
                s = lax.dot_general(kt, q_ab, nt, preferred_element_type=F32)
                p = jnp.exp2(s).astype(BF16)
                return (carry[0] + jnp.dot(va, p[:, :tq], preferred_element_type=F32),
                        carry[1] + jnp.dot(vb, p[:, tq:], preferred_element_type=F32))

            zero = jnp.zeros((LANES, tq), F32)
            acc_a, acc_b = lax.fori_loop(0, seq_len // tk, body, (zero, zero), unroll=True)
            finish(m, acc_a, acc_b)

    @pl.when(bounded_ref[0] == 0)
    def _():
        for m in range(Q_PER_KV):
            q_a, q_b = head_pair(m)

            def body(t, carry):
                kt, va, vb = tiles(t, tk_shifted)
                new = []
                for qh, vh, (mx, acc) in zip((q_a, q_b), (va, vb), carry):
                    s = lax.dot_general(kt, qh, nt, preferred_element_type=F32)
                    mx_new = jnp.maximum(mx, jnp.max(s, axis=0, keepdims=True))
                    p = jnp.exp2(s - mx_new).astype(BF16)
                    acc_new = (jnp.exp2(mx - mx_new) * acc
                               + jnp.dot(vh, p, preferred_element_type=F32))
                    new.append((mx_new, acc_new))
                return tuple(new)

            init = tuple((jnp.full((1, tq), -jnp.inf, F32), jnp.zeros((LANES, tq), F32))
                         for _ in range(2))
            (_, acc_a), (_, acc_b) = lax.fori_loop(0, seq_len // tk_shifted, body, init)
            finish(m, acc_a, acc_b)


def _attention(bounded, q, k, vt, batch, seq_len, tq, tk):
    n = q.shape[0]
    q_tiles = seq_len // tq
    pair_w = Q_PER_KV * LANES
    qmap = lambda b, j, i: (b * q_tiles + i, j)
    return pl.pallas_call(
        functools.partial(_attention_kernel, tk=tk),
        grid=(batch, KV_PAIRS, q_tiles),
        in_specs=[pl.BlockSpec(memory_space=pltpu.SMEM),
                  pl.BlockSpec((tq, pair_w), qmap),
                  pl.BlockSpec((seq_len, LANES), lambda b, j, i: (b, j)),
                  pl.BlockSpec((2 * LANES, seq_len), lambda b, j, i: (j, b))],
        out_specs=pl.BlockSpec((tq, pair_w), qmap),
        out_shape=jax.ShapeDtypeStruct((n, Q_WIDTH), BF16),
        compiler_params=_cparams("parallel", "parallel", "parallel"), name="attention",
    )(bounded, q, k, vt)


def _cmul(ar, ai, xr, xi):
    return ar * xr - ai * xi, ar * xi + ai * xr


def _s5_kernel(u_ref, a_ref, bin_ref, cout_ref, dec_ref, y_ref, x_ref, s_ref, ya_ref):
    n_chunks = x_ref.shape[0]
    n_tiles = n_chunks // SUBLANES
    row_block = min(S5_ROW_BLOCK, n_chunks)
    for i in range(CHUNK):
        x_ref[:, i * LANES:(i + 1) * LANES] = (
            u_ref[pl.ds(i, n_chunks, stride=CHUNK), :].astype(BF16))
    for r in range(n_chunks // row_block):
        rs = slice(r * row_block, (r + 1) * row_block)
        s_ref[rs, :] = jnp.dot(x_ref[rs, :], bin_ref[0], preferred_element_type=F32)
    f_re, f_im = slice(0, STATE_PART), slice(STATE_PART, 2 * STATE_PART)
    b_re, b_im = slice(2 * STATE_PART, 3 * STATE_PART), slice(3 * STATE_PART, 4 * STATE_PART)
    rows = lax.broadcasted_iota(jnp.int32, (SUBLANES, STATE_PART), 0)

    def scan_tile(xr, xi, cr, ci, re, im, down):
        for k in range(3):
            shift = (1 << k) if down else SUBLANES - (1 << k)
            pr, pi = _cmul(dec_ref[0, k, :, re], dec_ref[0, k, :, im],
                           pltpu.roll(xr, shift, 0), pltpu.roll(xi, shift, 0))
            xr, xi = xr + pr, xi + pi
        one = 1 if down else SUBLANES - 1
        edge = 0 if down else SUBLANES - 1
        er, ei = _cmul(dec_ref[0, 3, :, re], dec_ref[0, 3, :, im], cr, ci)
        er = er + jnp.where(rows == edge, 0.0, pltpu.roll(xr, one, 0))
        ei = ei + jnp.where(rows == edge, 0.0, pltpu.roll(xi, one, 0))
        last = SUBLANES - 1 - edge
        nr, ni = _cmul(dec_ref[0, 4, 0:1, re], dec_ref[0, 4, 0:1, im], cr, ci)
        return er, ei, nr + xr[last:last + 1, :], ni + xi[last:last + 1, :]

    for r in range(n_chunks // row_block):
        rs = slice(r * row_block, (r + 1) * row_block)
        ya_ref[rs, :] = jnp.dot(x_ref[rs, :], a_ref[0], preferred_element_type=F32)
    hr = hi = gr = gi = jnp.zeros((1, STATE_PART), F32)
    for t in range(n_tiles):
        rf = slice(t * SUBLANES, (t + 1) * SUBLANES)
        rb = slice((n_tiles - 1 - t) * SUBLANES, (n_tiles - t) * SUBLANES)
        er, ei, hr, hi = scan_tile(s_ref[rf, f_re], s_ref[rf, f_im], hr, hi, f_re, f_im, True)
        s_ref[rf, f_re] = er
        s_ref[rf, f_im] = ei
        er, ei, gr, gi = scan_tile(s_ref[rb, b_re], s_ref[rb, b_im], gr, gi, b_re, b_im, False)
        s_ref[rb, b_re] = er
        s_ref[rb, b_im] = ei
    for r in range(n_chunks // row_block):
        rs = slice(r * row_block, (r + 1) * row_block)
        y = ya_ref[rs, :] + jnp.dot(s_ref[rs, :].astype(BF16), cout_ref[0],
                                    preferred_element_type=F32)
        for i in range(CHUNK):
            y_ref[pl.ds(r * row_block * CHUNK + i, row_block, stride=CHUNK), :] = (
                y[:, i * LANES:(i + 1) * LANES])


def _s5(u, lw, batch, seq_len):
    n_chunks = seq_len // CHUNK
    w3 = lambda b, s: (b, 0, 0)
    io = pl.BlockSpec((seq_len, LANES), lambda b, s: (s, b))
    return pl.pallas_call(
        _s5_kernel, grid=(N_CHANNEL_BLOCKS, batch),
        in_specs=[io,
                  pl.BlockSpec((1, CHUNK_COLS, CHUNK_COLS), w3),
                  pl.BlockSpec((1, CHUNK_COLS, STATE_COLS), w3),
                  pl.BlockSpec((1, STATE_COLS, CHUNK_COLS), w3),
                  pl.BlockSpec((1, 5, SUBLANES, STATE_COLS), lambda b, s: (b, 0, 0, 0))],
        out_specs=io,
        out_shape=jax.ShapeDtypeStruct(u.shape, F32),
        scratch_shapes=[pltpu.VMEM((n_chunks, CHUNK_COLS), BF16),
                        pltpu.VMEM((n_chunks, STATE_COLS), F32),
                        pltpu.VMEM((n_chunks, CHUNK_COLS), F32)],
        compiler_params=_cparams("parallel", "parallel"), name="s5",
    )(u, lw["s5_a"], lw["s5_bin"], lw["s5_cout"], lw["s5_dec"])


def _s5_operators(lam_re, lam_im, log_step, b_re, b_im, c_re, c_im, d_skip):
    t = CHUNK
    lam = lax.complex(lam_re.astype(F32), lam_im.astype(F32))
    step = jnp.exp(log_step.astype(F32))
    lam_dt = lam * step[..., None]
    lam_bar = jnp.exp(lam_dt)
    b_bar = ((lam_bar - 1.0) / lam)[..., None] * lax.complex(b_re.astype(F32), b_im.astype(F32))
    c = lax.complex(c_re.astype(F32), c_im.astype(F32))
    powers = jnp.exp(lam_dt[None] * jnp.arange(t + 1, dtype=F32)[:, None, None, None])
    taps = jnp.einsum("dgop,ndgp,dgpi->ndgoi", c, powers[:t], b_bar).real
    idx = jnp.arange(t)
    lag = idx[None, :] - idx[:, None]
    fwd = jnp.where((lag >= 0)[:, :, None, None, None], taps[jnp.clip(lag, 0, t - 1), 0], 0.0)
    bwd = jnp.where((lag <= 0)[:, :, None, None, None], taps[jnp.clip(-lag, 0, t - 1), 1], 0.0)
    k_ji = fwd + bwd
    eye_t = jnp.eye(t, dtype=F32)
    eye_c = jnp.eye(SSM_GROUP, dtype=F32)
    skip = (eye_t[:, :, None, None, None] * eye_c[None, None, None]
            * d_skip.astype(F32).reshape(N_SSM_GROUPS, SSM_GROUP)[None, None, :, :, None])
    nb, gb = N_CHANNEL_BLOCKS, GROUPS_PER_BLOCK

    def block_diagonal(z, rows, cols):
        r2, c1, c2 = z.shape[3], z.shape[4], z.shape[5]
        compact = z.astype(BF16).transpose(0, 2, 1, 3, 4, 5).reshape(nb, rows, c1 * c2)
        src = np.arange(cols) // (gb * c2) * c2 + np.arange(cols) % c2
        spread = jnp.asarray(np.arange(c1 * c2)[:, None] == src[None, :], BF16)
        wide = jnp.einsum("brk,kc->brc", compact, spread, preferred_element_type=F32)
        row_g = lax.broadcasted_iota(jnp.int32, (rows, cols), 0) // r2 % gb
        col_g = lax.broadcasted_iota(jnp.int32, (rows, cols), 1) // c2 % gb
        return jnp.where(row_g == col_g, wide, 0.0).astype(BF16)

    a_grp = (k_ji + skip).transpose(2, 0, 4, 1, 3).reshape(nb, gb, t, SSM_GROUP, t, SSM_GROUP)
    a_op = block_diagonal(a_grp, CHUNK_COLS, CHUNK_COLS)
    in_f = jnp.einsum("jgp,gpi->gjip", powers[t - 1 - idx, 0], b_bar[0])
    in_b = jnp.einsum("jgp,gpi->gjip", powers[idx, 1], b_bar[1])
    bin_grp = jnp.stack([in_f.real, in_f.imag, in_b.real, in_b.imag], axis=3)
    bin_op = block_diagonal(bin_grp.reshape(nb, gb, t, SSM_GROUP, 4, SSM_STATE),
                            CHUNK_COLS, STATE_COLS)
    out_f = jnp.einsum("gop,igp->gpio", c[0], powers[idx + 1, 0])
    out_b = jnp.einsum("gop,igp->gpio", c[1], powers[t - idx, 1])
    cout_grp = jnp.stack([out_f.real, -out_f.imag, out_b.real, -out_b.imag], axis=1)
    cout_op = block_diagonal(cout_grp.reshape(nb, gb, 4, SSM_STATE, t, SSM_GROUP),
                             STATE_COLS, CHUNK_COLS)
    r = jnp.arange(SUBLANES)
    tile_pow = jnp.exp(lam_dt[None] * (t * jnp.arange(SUBLANES + 1, dtype=F32))[:, None, None, None])

    def scan_consts(d, down):
        z = []
        for k in (1, 2, 4):
            keep = (r >= k) if down else (r <= SUBLANES - 1 - k)
            z.append(jnp.where(keep[:, None, None], tile_pow[k, d][None], 0.0))
        z.append(tile_pow[r if down else SUBLANES - 1 - r, d])
        z.append(jnp.broadcast_to(tile_pow[SUBLANES, d][None], z[0].shape))
        return jnp.stack(z)

    zf, zb = scan_consts(0, True), scan_consts(1, False)
    dec = jnp.stack([zf.real, zf.imag, zb.real, zb.imag], axis=2)
    dec = dec.reshape(5, SUBLANES, 4, nb, gb * SSM_STATE).transpose(3, 0, 1, 2, 4)
    dec = dec.reshape(nb, 5, SUBLANES, STATE_COLS)
    return a_op, bin_op, cout_op, dec


def _mix_kernel(x_ref, attn_ref, y_ref, ga_ref, gs_ref, wp_ref, wa_ref, wb_ref, wo_ref, o_ref):
    attn_out = jnp.dot(attn_ref[...], wp_ref[...], preferred_element_type=F32)
    yg = jax.nn.gelu(y_ref[...]).astype(BF16)
    ssm_out = (jnp.dot(yg, wa_ref[...], preferred_element_type=F32)
               * jax.nn.sigmoid(jnp.dot(yg, wb_ref[...], preferred_element_type=F32)))
    merged = ga_ref[...].astype(F32) * attn_out + gs_ref[...].astype(F32) * ssm_out
    o_ref[...] = x_ref[...] + jnp.dot(merged.astype(BF16), wo_ref[...],
                                      preferred_element_type=F32)


def _mix(x2, attn, y, ga, gs, lw, tm):
    n = x2.shape[0]
    row = lambda i: (i, 0)
    full = lambda a: pl.BlockSpec(a.shape, lambda i: (0, 0))
    return pl.pallas_call(
        _mix_kernel, grid=(n // tm,),
        in_specs=[pl.BlockSpec((tm, D_MODEL), row), pl.BlockSpec((tm, Q_WIDTH), row),
                  pl.BlockSpec((tm, SSM_WIDTH), row), pl.BlockSpec((tm, D_MODEL), row),
                  pl.BlockSpec((tm, D_MODEL), row), full(lw["wp"]), full(lw["wa"]),
                  full(lw["wb"]), full(lw["wo"])],
        out_specs=pl.BlockSpec((tm, D_MODEL), row),
        out_shape=jax.ShapeDtypeStruct((n, D_MODEL), F32),
        compiler_params=_cparams("parallel"), name="mix",
    )(x2, attn, y, ga, gs, lw["wp"], lw["wa"], lw["wb"], lw["wo"])


def _mlp_kernel(x_ref, gain_ref, w1_ref, w2_ref, o_ref):
    x = x_ref[...]
    h = x * lax.rsqrt(jnp.mean(x * x, axis=-1, keepdims=True) + EPS) * gain_ref[...]
    a = jnp.dot(h.astype(BF16), w1_ref[...], preferred_element_type=F32)
    a = jnp.square(jnp.maximum(a, 0.0)).astype(BF16)
    o_ref[...] = x + jnp.dot(a, w2_ref[...], preferred_element_type=F32)


def _mlp(x2, lw, tm):
    n = x2.shape[0]
    row = lambda i: (i, 0)
    full = lambda a: pl.BlockSpec(a.shape, lambda i: (0, 0))
    return pl.pallas_call(
        _mlp_kernel, grid=(n // tm,),
        in_specs=[pl.BlockSpec((tm, D_MODEL), row), full(lw["norm_mlp"]), full(lw["w1"]),
                  full(lw["w2"])],
        out_specs=pl.BlockSpec((tm, D_MODEL), row),
        out_shape=jax.ShapeDtypeStruct((n, D_MODEL), F32),
        compiler_params=_cparams("parallel"), name="mlp",
    )(x2, lw["norm_mlp"], lw["w1"], lw["w2"])


def _q_permutation():
    perm = []
    for j in range(KV_PAIRS):
        for m in range(Q_PER_KV):
            for half in range(2):
                head = Q_PER_KV * (2 * j + half) + m
                perm.extend(range(head * HEAD_DIM, (head + 1) * HEAD_DIM))
    return np.asarray(perm, dtype=np.int32)


def _rope_tables(seq_len):
    t = jnp.arange(seq_len)
    inv_freq = ROPE_THETA ** (-jnp.arange(0, AXIS_DIM, 2, dtype=F32) / AXIS_DIM)
    ang = jnp.concatenate([(t // GRID_W).astype(F32)[:, None] * inv_freq[None, :],
                           (t % GRID_W).astype(F32)[:, None] * inv_freq[None, :]], axis=-1)
    cos = jnp.repeat(jnp.cos(ang), 2, axis=-1)
    sin = jnp.repeat(jnp.sin(ang), 2, axis=-1)
    even = (jnp.arange(HEAD_DIM) % 2 == 0)[None, :]
    sa = jnp.where(even, -sin, 0.0)
    sb = jnp.where(even, 0.0, sin)
    two = lambda a: jnp.concatenate([a, a], axis=-1)
    return two(cos), two(sa), two(sb)


def _scores_bounded(q_gain, k_gain):
    bound = (1.05 * HEAD_DIM * Q_SCALE * jnp.max(jnp.abs(q_gain.astype(F32)))
             * jnp.max(jnp.abs(k_gain.astype(F32))))
    return (bound <= MAX_UNSHIFTED_SCORE).astype(jnp.int32).reshape(1)


def _layer_weights(i, p):
    perm = _q_permutation()
    w_in = p["w_in"][i]
    o = 0
    wq = w_in[:, o:o + Q_WIDTH]; o += Q_WIDTH
    wk = w_in[:, o:o + KV_WIDTH]; o += KV_WIDTH
    wv = w_in[:, o:o + KV_WIDTH]; o += KV_WIDTH
    wu = w_in[:, o:o + SSM_WIDTH]; o += SSM_WIDTH
    wga = w_in[:, o:o + D_MODEL]; o += D_MODEL
    wgs = w_in[:, o:o + D_MODEL]
    two =lambda a: jnp.concatenate([a, a])[None, :].astype(F32)
    a_op, bin_op, cout_op, dec = (z[i] for z in p["s5_ops"])
    return {
        "norm_mix": p["norm_mix"][i][None, :].astype(F32),
        "wq": wq[:, perm].astype(BF16), "wk": wk.astype(BF16), "wvt": wv.T.astype(BF16),
        "wu": wu.astype(BF16), "wga": wga.astype(BF16), "wgs": wgs.astype(BF16),
        "qn": two(p["q_norm"][i]), "kn": two(p["k_norm"][i]),
        "bounded": _scores_bounded(p["q_norm"][i], p["k_norm"][i]),
        "s5_a": a_op, "s5_bin": bin_op, "s5_cout": cout_op, "s5_dec": dec,
        "wp": p["w_attn_proj"][i][perm, :].astype(BF16),
        "wa": p["w_glu_a"][i].astype(BF16), "wb": p["w_glu_b"][i].astype(BF16),
        "wo": p["w_out"][i].astype(BF16),
        "norm_mlp": p["norm_mlp"][i][None, :].astype(F32),
        "w1": p["w_ff1"][i].astype(BF16), "w2": p["w_ff2"][i].astype(BF16),
    }


def _tiles(seq_len):
    assert seq_len % (CHUNK * SUBLANES) == 0 and seq_len % LANES == 0
    prefs = dict(proj_rows=512, mlp_rows=256, queries=512, keys=4096)
    tiles = {name: min(seq_len, pref) for name, pref in prefs.items()}
    assert all(seq_len % t == 0 for t in tiles.values()), (seq_len, tiles)
    return tiles


def _trunk(x, layers):
    batch, seq_len, _ = x.shape
    n = batch * seq_len
    rope = _rope_tables(seq_len)
    t = _tiles(seq_len)
    x2 = x.reshape(n, D_MODEL)
    for lw in layers:
        q, k, vt, u, ga, gs = _in_proj(x2, lw, rope, seq_len, t["proj_rows"])
        attn = _attention(lw["bounded"], q, k, vt, batch, seq_len, t["queries"], t["keys"])
        y = _s5(u, lw, batch, seq_len)
        x2 = _mix(x2, attn, y, ga, gs, lw, t["proj_rows"])
        x2 = _mlp(x2, lw, t["mlp_rows"])
    return x2.reshape(batch, seq_len, D_MODEL)


def kernel(x_prompt, x_sample, norm_mix, w_in, q_norm, k_norm, w_attn_proj, ssm_lambda_re, ssm_lambda_im, ssm_log_step, ssm_b_re, ssm_b_im, ssm_c_re, ssm_c_im, ssm_d, w_glu_a, w_glu_b, w_out, norm_mlp, w_ff1, w_ff2):
    params = dict(norm_mix=norm_mix, w_in=w_in, q_norm=q_norm, k_norm=k_norm,
                  w_attn_proj=w_attn_proj, ssm_lambda_re=ssm_lambda_re,
                  ssm_lambda_im=ssm_lambda_im, ssm_log_step=ssm_log_step, ssm_b_re=ssm_b_re,
                  ssm_b_im=ssm_b_im, ssm_c_re=ssm_c_re, ssm_c_im=ssm_c_im, ssm_d=ssm_d,
                  w_glu_a=w_glu_a, w_glu_b=w_glu_b, w_out=w_out, norm_mlp=norm_mlp,
                  w_ff1=w_ff1, w_ff2=w_ff2)
    params["s5_ops"] = jax.vmap(_s5_operators)(
        ssm_lambda_re, ssm_lambda_im, ssm_log_step, ssm_b_re, ssm_b_im, ssm_c_re, ssm_c_im, ssm_d)
    layers =[_layer_weights(i, params) for i in range(norm_mix.shape[0])]
    return (_trunk(x_prompt, layers), _trunk(x_sample, layers))
```

```python
import functools
import math

import jax
import jax.numpy as jnp
import numpy as np
from jax import lax
from jax.experimental import pallas as pl
from jax.experimental.pallas import tpu as pltpu

D_MODEL = 1024
GRID_W = 64
N_Q_HEADS = 16
N_KV_HEADS = 4
HEAD_DIM = 64
ROPE_THETA = 10000.0
AXIS_DIM = HEAD_DIM // 2
Q_WIDTH = N_Q_HEADS * HEAD_DIM
KV_WIDTH = N_KV_HEADS * HEAD_DIM
SSM_WIDTH = 512
SSM_GROUP = 16
N_SSM_GROUPS = SSM_WIDTH // SSM_GROUP
SSM_STATE = 64
D_FF = 4 * D_MODEL
EPS = 1e-6

LANES = 128
SUBLANES = 8
CHUNK = 8
GROUPS_PER_BLOCK = LANES // SSM_GROUP
N_CHANNEL_BLOCKS = SSM_WIDTH // LANES
CHUNK_COLS = CHUNK * LANES
STATE_PART = GROUPS_PER_BLOCK * SSM_STATE
STATE_COLS = 4 * STATE_PART
S5_ROW_BLOCK = 256
KV_PAIRS = N_KV_HEADS // 2
Q_PER_KV = N_Q_HEADS // N_KV_HEADS
VT_ROWS = KV_PAIRS * 4 * HEAD_DIM
LOG2E = math.log2(math.e)
Q_SCALE = HEAD_DIM ** -0.5 * LOG2E
MAX_UNSHIFTED_SCORE = 100.0
SHIFTED_KEY_TILE = 512
VMEM_LIMIT = 56 * 1024 * 1024

BF16 = jnp.bfloat16
F32 = jnp.float32


def _cparams(*sem):
    return pltpu.CompilerParams(dimension_semantics=sem, vmem_limit_bytes=VMEM_LIMIT)


def _in_proj_kernel(x_ref, gain_ref, wq_ref, wk_ref, wvt_ref, wu_ref, wga_ref, wgs_ref,
                    qn_ref, kn_ref, cos_ref, sa_ref, sb_ref,
                    q_ref, k_ref, vt_ref, u_ref, ga_ref, gs_ref):
    x = x_ref[...]
    xn = x * lax.rsqrt(jnp.mean(x * x, axis=-1, keepdims=True) + EPS) * gain_ref[...]
    xn = xn.astype(BF16)
    cos = cos_ref[...]
    sa = sa_ref[...]
    sb = sb_ref[...]

    left_head = lax.broadcasted_iota(jnp.int32, (x.shape[0], LANES), 1) < HEAD_DIM

    def head_norm_rope(blk, gain, scale):
        sq = blk * blk
        both = jnp.sum(sq, axis=-1, keepdims=True)
        left = jnp.sum(jnp.where(left_head, sq, 0.0), axis=-1, keepdims=True)
        msq = jnp.where(left_head, left, both - left) * (1.0 / HEAD_DIM)
        y = blk * lax.rsqrt(msq + EPS) * gain
        y = y * cos + pltpu.roll(y, LANES - 1, 1) * sa + pltpu.roll(y, 1, 1) * sb
        return y * scale

    q = jnp.dot(xn, wq_ref[...], preferred_element_type=F32)
    for b in range(Q_WIDTH // LANES):
        sl = slice(b * LANES, (b + 1) * LANES)
        q_ref[:, sl] = head_norm_rope(q[:, sl], qn_ref[...], Q_SCALE).astype(BF16)
    k = jnp.dot(xn, wk_ref[...], preferred_element_type=F32)
    for b in range(KV_WIDTH // LANES):
        sl = slice(b * LANES, (b + 1) * LANES)
        k_ref[:, sl] = head_norm_rope(k[:, sl], kn_ref[...], 1.0).astype(BF16)
    vt = lax.dot_general(wvt_ref[...], xn, (((1,), (1,)), ((), ())), preferred_element_type=F32)
    vt = vt.astype(BF16)
    ones = jnp.ones((2 * HEAD_DIM, vt.shape[1]), BF16)
    for j in range(KV_PAIRS):
        src, dst = 2 * j * HEAD_DIM, 4 * j * HEAD_DIM
        vt_ref[dst:dst + HEAD_DIM, :] = vt[src:src + HEAD_DIM, :]
        vt_ref[dst + HEAD_DIM:dst + 3 * HEAD_DIM, :] = ones
        vt_ref[dst + 3 * HEAD_DIM:dst + 4 * HEAD_DIM, :] = vt[src + HEAD_DIM:src + 2 * HEAD_DIM, :]
    u_ref[...] = jnp.dot(xn, wu_ref[...], preferred_element_type=F32)
    ga = jnp.dot(xn, wga_ref[...], preferred_element_type=F32)
    ga_ref[...] = jax.nn.sigmoid(ga).astype(BF16)
    gs = jnp.dot(xn, wgs_ref[...], preferred_element_type=F32)
    gs_ref[...] = jax.nn.sigmoid(gs).astype(BF16)


def _in_proj(x2, lw, rope, seq_len, tm):
    n = x2.shape[0]
    tiles_per_seq = seq_len // tm
    row = lambda i: (i, 0)
    const = lambda i: (0, 0)
    pos = lambda i: (i % tiles_per_seq, 0)
    full = lambda a: pl.BlockSpec(a.shape, const)
    cos, sa, sb = rope
    in_specs = [
        pl.BlockSpec((tm, D_MODEL), row), full(lw["norm_mix"]),
        full(lw["wq"]), full(lw["wk"]), full(lw["wvt"]), full(lw["wu"]), full(lw["wga"]),
        full(lw["wgs"]), full(lw["qn"]), full(lw["kn"]),
        pl.BlockSpec((tm, LANES), pos), pl.BlockSpec((tm, LANES), pos),
        pl.BlockSpec((tm, LANES), pos),
    ]
    out_shape = [
        jax.ShapeDtypeStruct((n, Q_WIDTH), BF16), jax.ShapeDtypeStruct((n, KV_WIDTH), BF16),
        jax.ShapeDtypeStruct((VT_ROWS, n), BF16), jax.ShapeDtypeStruct((n, SSM_WIDTH), F32),
        jax.ShapeDtypeStruct((n, D_MODEL), BF16), jax.ShapeDtypeStruct((n, D_MODEL), BF16),
    ]
    out_specs = [
        pl.BlockSpec((tm, Q_WIDTH), row), pl.BlockSpec((tm, KV_WIDTH), row),
        pl.BlockSpec((VT_ROWS, tm), lambda i: (0, i)), pl.BlockSpec((tm, SSM_WIDTH), row),
        pl.BlockSpec((tm, D_MODEL), row), pl.BlockSpec((tm, D_MODEL), row),
    ]
    return pl.pallas_call(
        _in_proj_kernel, grid=(n // tm,), in_specs=in_specs, out_specs=out_specs,
        out_shape=out_shape, compiler_params=_cparams("parallel"), name="in_proj",
    )(x2, lw["norm_mix"], lw["wq"], lw["wk"], lw["wvt"], lw["wu"], lw["wga"], lw["wgs"],
      lw["qn"], lw["kn"], cos, sa, sb)


def _attention_kernel(bounded_ref, q_ref, k_ref, vt_ref, o_ref, *, tk):
    tq = q_ref.shape[0]
    seq_len = k_ref.shape[0]
    tk_shifted = min(tk, SHIFTED_KEY_TILE)
    lane = lax.broadcasted_iota(jnp.int32, (tq, LANES), 1)
    row = lax.broadcasted_iota(jnp.int32, (LANES, tq), 0)
    nt = (((1,), (1,)), ((), ()))

    def head_pair(m):
        qm = q_ref[:, m * LANES:(m + 1) * LANES]
        return (jnp.where(lane < HEAD_DIM, qm, jnp.zeros_like(qm)),
                jnp.where(lane >= HEAD_DIM, qm, jnp.zeros_like(qm)))

    def tiles(t, width):
        start = pl.multiple_of(t * width, width)
        return (k_ref[pl.ds(start, width), :], vt_ref[0:LANES, pl.ds(start, width)],
                vt_ref[LANES:2 * LANES, pl.ds(start, width)])

    def finish(m, acc_a, acc_b):
        o_t = jnp.where(row < HEAD_DIM, acc_a / acc_a[HEAD_DIM:HEAD_DIM + 1, :],
                        acc_b / acc_b[0:1, :])
        o_ref[:, m * LANES:(m + 1) * LANES] = o_t.T.astype(BF16)

    @pl.when(bounded_ref[0] != 0)
    def _():
        for m in range(Q_PER_KV):
            q_a, q_b = head_pair(m)

            q_ab = jnp.concatenate([q_a, q_b], axis=0)

            def body(t, carry):
                kt, va, vb = tiles(t, tk)
                s = lax.dot_general(kt, q_ab, nt, preferred_element_type=F32)
                p = jnp.exp2(s).astype(BF16)
                return (carry[0] + jnp.dot(va, p[:, :tq], preferred_element_type=F32),
                        carry[1] + jnp.dot(vb, p[:, tq:], preferred_element_type=F32))

            zero = jnp.zeros((LANES, tq), F32)
            acc_a, acc_b = lax.fori_loop(0, seq_len // tk, body, (zero, zero), unroll=True)
            finish(m, acc_a, acc_b)

    @pl.when(bounded_ref[0] == 0)
    def _():
        for m in range(Q_PER_KV):
            q_a, q_b = head_pair(m)

            def body(t, carry):
                kt, va, vb = tiles(t, tk_shifted)
                new = []
                for qh, vh, (mx, acc) in zip((q_a, q_b), (va, vb), carry):
                    s = lax.dot_general(kt, qh, nt, preferred_element_type=F32)
                    mx_new = jnp.maximum(mx, jnp.max(s, axis=0, keepdims=True))
                    p = jnp.exp2(s - mx_new).astype(BF16)
                    acc_new = (jnp.exp2(mx - mx_new) * acc
                               + jnp.dot(vh, p, preferred_element_type=F32))
                    new.append((mx_new, acc_new))
                return tuple(new)

            init = tuple((jnp.full((1, tq), -jnp.inf, F32), jnp.zeros((LANES, tq), F32))
                         for _ in range(2))
            (_, acc_a), (_, acc_b) = lax.fori_loop(0, seq_len // tk_shifted, body, init)
            finish(m, acc_a, acc_b)


def _attention(bounded, q, k, vt, batch, seq_len, tq, tk):
    n = q.shape[0]
    q_tiles = seq_len // tq
    pair_w = Q_PER_KV * LANES
    qmap = lambda b, j, i: (b * q_tiles + i, j)
    return pl.pallas_call(
        functools.partial(_attention_kernel, tk=tk),
        grid=(batch, KV_PAIRS, q_tiles),
        in_specs=[pl.BlockSpec(memory_space=pltpu.SMEM),
                  pl.BlockSpec((tq, pair_w), qmap),
                  pl.BlockSpec((seq_len, LANES), lambda b, j, i: (b, j)),
                  pl.BlockSpec((2 * LANES, seq_len), lambda b, j, i: (j, b))],
        out_specs=pl.BlockSpec((tq, pair_w), qmap),
        out_shape=jax.ShapeDtypeStruct((n, Q_WIDTH), BF16),
        compiler_params=_cparams("parallel", "parallel", "parallel"), name="attention",
    )(bounded, q, k, vt)


def _cmul(ar, ai, xr, xi):
    return ar * xr - ai * xi, ar * xi + ai * xr


def _s5_kernel(u_ref, a_ref, bin_ref, cout_ref, dec_ref, y_ref, x_ref, s_ref, ya_ref):
    n_chunks = x_ref.shape[0]
    n_tiles = n_chunks // SUBLANES
    row_block = min(S5_ROW_BLOCK, n_chunks)
    for i in range(CHUNK):
        x_ref[:, i * LANES:(i + 1) * LANES] = (
            u_ref[pl.ds(i, n_chunks, stride=CHUNK), :].astype(BF16))
    for r in range(n_chunks // row_block):
        rs = slice(r * row_block, (r + 1) * row_block)
        s_ref[rs, :] = jnp.dot(x_ref[rs, :], bin_ref[0], preferred_element_type=F32)
    f_re, f_im = slice(0, STATE_PART), slice(STATE_PART, 2 * STATE_PART)
    b_re, b_im = slice(2 * STATE_PART, 3 * STATE_PART), slice(3 * STATE_PART, 4 * STATE_PART)
    rows = lax.broadcasted_iota(jnp.int32, (SUBLANES, STATE_PART), 0)

    def scan_tile(xr, xi, cr, ci, re, im, down):
        for k in range(3):
            shift = (1 << k) if down else SUBLANES - (1 << k)
            pr, pi = _cmul(dec_ref[0, k, :, re], dec_ref[0, k, :, im],
                           pltpu.roll(xr, shift, 0), pltpu.roll(xi, shift, 0))
            xr, xi = xr + pr, xi + pi
        one = 1 if down else SUBLANES - 1
        edge = 0 if down else SUBLANES - 1
        er, ei = _cmul(dec_ref[0, 3, :, re], dec_ref[0, 3, :, im], cr, ci)
        er = er + jnp.where(rows == edge, 0.0, pltpu.roll(xr, one, 0))
        ei = ei + jnp.where(rows == edge, 0.0, pltpu.roll(xi, one, 0))
        last = SUBLANES - 1 - edge
        nr, ni = _cmul(dec_ref[0, 4, 0:1, re], dec_ref[0, 4, 0:1, im], cr, ci)
        return er, ei, nr + xr[last:last + 1, :], ni + xi[last:last + 1, :]

    for r in range(n_chunks // row_block):
        rs = slice(r * row_block, (r + 1) * row_block)
        ya_ref[rs, :] = jnp.dot(x_ref[rs, :], a_ref[0], preferred_element_type=F32)
    hr = hi = gr = gi = jnp.zeros((1, STATE_PART), F32)
    for t in range(n_tiles):
        rf = slice(t * SUBLANES, (t + 1) * SUBLANES)
        rb = slice((n_tiles - 1 - t) * SUBLANES, (n_tiles - t) * SUBLANES)
        er, ei, hr, hi = scan_tile(s_ref[rf, f_re], s_ref[rf, f_im], hr, hi, f_re, f_im, True)
        s_ref[rf, f_re] = er
        s_ref[rf, f_im] = ei
        er, ei, gr, gi = scan_tile(s_ref[rb, b_re], s_ref[rb, b_im], gr, gi, b_re, b_im, False)
        s_ref[rb, b_re] = er
        s_ref[rb, b_im] = ei
    for r in range(n_chunks // row_block):
        rs = slice(r * row_block, (r + 1) * row_block)
        y = ya_ref[rs, :] + jnp.dot(s_ref[rs, :].astype(BF16), cout_ref[0],
                                    preferred_element_type=F32)
        for i in range(CHUNK):
            y_ref[pl.ds(r * row_block * CHUNK + i, row_block, stride=CHUNK), :] = (
                y[:, i * LANES:(i + 1) * LANES])


def _s5(u, lw, batch, seq_len):
    n_chunks = seq_len // CHUNK
    w3 = lambda b, s: (b, 0, 0)
    io = pl.BlockSpec((seq_len, LANES), lambda b, s: (s, b))
    return pl.pallas_call(
        _s5_kernel, grid=(N_CHANNEL_BLOCKS, batch),
        in_specs=[io,
                  pl.BlockSpec((1, CHUNK_COLS, CHUNK_COLS), w3),
                  pl.BlockSpec((1, CHUNK_COLS, STATE_COLS), w3),
                  pl.BlockSpec((1, STATE_COLS, CHUNK_COLS), w3),
                  pl.BlockSpec((1, 5, SUBLANES, STATE_COLS), lambda b, s: (b, 0, 0, 0))],
        out_specs=io,
        out_shape=jax.ShapeDtypeStruct(u.shape, F32),
        scratch_shapes=[pltpu.VMEM((n_chunks, CHUNK_COLS), BF16),
                        pltpu.VMEM((n_chunks, STATE_COLS), F32),
                        pltpu.VMEM((n_chunks, CHUNK_COLS), F32)],
        compiler_params=_cparams("parallel", "parallel"), name="s5",
    )(u, lw["s5_a"], lw["s5_bin"], lw["s5_cout"], lw["s5_dec"])


def _s5_operators(lam_re, lam_im, log_step, b_re, b_im, c_re, c_im, d_skip):
    t = CHUNK
    lam = lax.complex(lam_re.astype(F32), lam_im.astype(F32))
    step = jnp.exp(log_step.astype(F32))
    lam_dt = lam * step[..., None]
    lam_bar = jnp.exp(lam_dt)
    b_bar = ((lam_bar - 1.0) / lam)[..., None] * lax.complex(b_re.astype(F32), b_im.astype(F32))
    c = lax.complex(c_re.astype(F32), c_im.astype(F32))
    powers = jnp.exp(lam_dt[None] * jnp.arange(t + 1, dtype=F32)[:, None, None, None])
    taps = jnp.einsum("dgop,ndgp,dgpi->ndgoi", c, powers[:t], b_bar).real
    idx = jnp.arange(t)
    lag = idx[None, :] - idx[:, None]
    fwd = jnp.where((lag >= 0)[:, :, None, None, None], taps[jnp.clip(lag, 0, t - 1), 0], 0.0)
    bwd = jnp.where((lag <= 0)[:, :, None, None, None], taps[jnp.clip(-lag, 0, t - 1), 1], 0.0)
    k_ji = fwd + bwd
    eye_t = jnp.eye(t, dtype=F32)
    eye_c = jnp.eye(SSM_GROUP, dtype=F32)
    skip = (eye_t[:, :, None, None, None] * eye_c[None, None, None]
            * d_skip.astype(F32).reshape(N_SSM_GROUPS, SSM_GROUP)[None, None, :, :, None])
    nb, gb = N_CHANNEL_BLOCKS, GROUPS_PER_BLOCK

    def block_diagonal(z, rows, cols):
        r2, c1, c2 = z.shape[3], z.shape[4], z.shape[5]
        compact = z.astype(BF16).transpose(0, 2, 1, 3, 4, 5).reshape(nb, rows, c1 * c2)
        src = np.arange(cols) // (gb * c2) * c2 + np.arange(cols) % c2
        spread = jnp.asarray(np.arange(c1 * c2)[:, None] == src[None, :], BF16)
        wide = jnp.einsum("brk,kc->brc", compact, spread, preferred_element_type=F32)
        row_g = lax.broadcasted_iota(jnp.int32, (rows, cols), 0) // r2 % gb
        col_g = lax.broadcasted_iota(jnp.int32, (rows, cols), 1) // c2 % gb
        return jnp.where(row_g == col_g, wide, 0.0).astype(BF16)

    a_grp = (k_ji + skip).transpose(2, 0, 4, 1, 3).reshape(nb, gb, t, SSM_GROUP, t, SSM_GROUP)
    a_op = block_diagonal(a_grp, CHUNK_COLS, CHUNK_COLS)
    in_f = jnp.einsum("jgp,gpi->gjip", powers[t - 1 - idx, 0], b_bar[0])
    in_b = jnp.einsum("jgp,gpi->gjip", powers[idx, 1], b_bar[1])
    bin_grp = jnp.stack([in_f.real, in_f.imag, in_b.real, in_b.imag], axis=3)
    bin_op = block_diagonal(bin_grp.reshape(nb, gb, t, SSM_GROUP, 4, SSM_STATE),
                            CHUNK_COLS, STATE_COLS)
    out_f = jnp.einsum("gop,igp->gpio", c[0], powers[idx + 1, 0])
    out_b = jnp.einsum("gop,igp->gpio", c[1], powers[t - idx, 1])
    cout_grp = jnp.stack([out_f.real, -out_f.imag, out_b.real, -out_b.imag], axis=1)
    cout_op = block_diagonal(cout_grp.reshape(nb, gb, 4, SSM_STATE, t, SSM_GROUP),
                             STATE_COLS, CHUNK_COLS)
    r = jnp.arange(SUBLANES)
    tile_pow = jnp.exp(lam_dt[None] * (t * jnp.arange(SUBLANES + 1, dtype=F32))[:, None, None, None])

    def scan_consts(d, down):
        z = []
        for k in (1, 2, 4):
            keep = (r >= k) if down else (r <= SUBLANES - 1 - k)
            z.append(jnp.where(keep[:, None, None], tile_pow[k, d][None], 0.0))
        z.append(tile_pow[r if down else SUBLANES - 1 - r, d])
        z.append(jnp.broadcast_to(tile_pow[SUBLANES, d][None], z[0].shape))
        return jnp.stack(z)

    zf, zb = scan_consts(0, True), scan_consts(1, False)
    dec = jnp.stack([zf.real, zf.imag, zb.real, zb.imag], axis=2)
    dec = dec.reshape(5, SUBLANES, 4, nb, gb * SSM_STATE).transpose(3, 0, 1, 2, 4)
    dec = dec.reshape(nb, 5, SUBLANES, STATE_COLS)
    return a_op, bin_op, cout_op, dec


def _mix_kernel(x_ref, attn_ref, y_ref, ga_ref, gs_ref, wp_ref, wa_ref, wb_ref, wo_ref, o_ref):
    attn_out = jnp.dot(attn_ref[...], wp_ref[...], preferred_element_type=F32)
    yg = jax.nn.gelu(y_ref[...]).astype(BF16)
    ssm_out = (jnp.dot(yg, wa_ref[...], preferred_element_type=F32)
               * jax.nn.sigmoid(jnp.dot(yg, wb_ref[...], preferred_element_type=F32)))
    merged = ga_ref[...].astype(F32) * attn_out + gs_ref[...].astype(F32) * ssm_out
    o_ref[...] = x_ref[...] + jnp.dot(merged.astype(BF16), wo_ref[...],
                                      preferred_element_type=F32)


def _mix(x2, attn, y, ga, gs, lw, tm):
    n = x2.shape[0]
    row = lambda i: (i, 0)
    full = lambda a: pl.BlockSpec(a.shape, lambda i: (0, 0))
    return pl.pallas_call(
        _mix_kernel, grid=(n // tm,),
        in_specs=[pl.BlockSpec((tm, D_MODEL), row), pl.BlockSpec((tm, Q_WIDTH), row),
                  pl.BlockSpec((tm, SSM_WIDTH), row), pl.BlockSpec((tm, D_MODEL), row),
                  pl.BlockSpec((tm, D_MODEL), row), full(lw["wp"]), full(lw["wa"]),
                  full(lw["wb"]), full(lw["wo"])],
        out_specs=pl.BlockSpec((tm, D_MODEL), row),
        out_shape=jax.ShapeDtypeStruct((n, D_MODEL), F32),
        compiler_params=_cparams("parallel"), name="mix",
    )(x2, attn, y, ga, gs, lw["wp"], lw["wa"], lw["wb"], lw["wo"])


def _mlp_kernel(x_ref, gain_ref, w1_ref, w2_ref, o_ref):
    x = x_ref[...]
    h = x * lax.rsqrt(jnp.mean(x * x, axis=-1, keepdims=True) + EPS) * gain_ref[...]
    a = jnp.dot(h.astype(BF16), w1_ref[...], preferred_element_type=F32)
    a = jnp.square(jnp.maximum(a, 0.0)).astype(BF16)
    o_ref[...] = x + jnp.dot(a, w2_ref[...], preferred_element_type=F32)


def _mlp(x2, lw, tm):
    n = x2.shape[0]
    row = lambda i: (i, 0)
    full = lambda a: pl.BlockSpec(a.shape, lambda i: (0, 0))
    return pl.pallas_call(
        _mlp_kernel, grid=(n // tm,),
        in_specs=[pl.BlockSpec((tm, D_MODEL), row), full(lw["norm_mlp"]), full(lw["w1"]),
                  full(lw["w2"])],
        out_specs=pl.BlockSpec((tm, D_MODEL), row),
        out_shape=jax.ShapeDtypeStruct((n, D_MODEL), F32),
        compiler_params=_cparams("parallel"), name="mlp",
    )(x2, lw["norm_mlp"], lw["w1"], lw["w2"])


def _q_permutation():
    perm = []
    for j in range(KV_PAIRS):
        for m in range(Q_PER_KV):
            for half in range(2):
                head = Q_PER_KV * (2 * j + half) + m
                perm.extend(range(head * HEAD_DIM, (head + 1) * HEAD_DIM))
    return np.asarray(perm, dtype=np.int32)


def _rope_tables(seq_len):
    t = jnp.arange(seq_len)
    inv_freq = ROPE_THETA ** (-jnp.arange(0, AXIS_DIM, 2, dtype=F32) / AXIS_DIM)
    ang = jnp.concatenate([(t // GRID_W).astype(F32)[:, None] * inv_freq[None, :],
                           (t % GRID_W).astype(F32)[:, None] * inv_freq[None, :]], axis=-1)
    cos = jnp.repeat(jnp.cos(ang), 2, axis=-1)
    sin = jnp.repeat(jnp.sin(ang), 2, axis=-1)
    even = (jnp.arange(HEAD_DIM) % 2 == 0)[None, :]
    sa = jnp.where(even, -sin, 0.0)
    sb = jnp.where(even, 0.0, sin)
    two = lambda a: jnp.concatenate([a, a], axis=-1)
    return two(cos), two(sa), two(sb)


def _scores_bounded(q_gain, k_gain):
    bound = (1.05 * HEAD_DIM * Q_SCALE * jnp.max(jnp.abs(q_gain.astype(F32)))
             * jnp.max(jnp.abs(k_gain.astype(F32))))
    return (bound <= MAX_UNSHIFTED_SCORE).astype(jnp.int32).reshape(1)


def _layer_weights(i, p):
    perm = _q_permutation()
    w_in = p["w_in"][i]
    o = 0
    wq = w_in[:, o:o + Q_WIDTH]; o += Q_WIDTH
    wk = w_in[:, o:o + KV_WIDTH]; o += KV_WIDTH
    wv = w_in[:, o:o + KV_WIDTH]; o += KV_WIDTH
    wu = w_in[:, o:o + SSM_WIDTH]; o += SSM_WIDTH
    wga = w_in[:, o:o + D_MODEL]; o += D_MODEL
    wgs = w_in[:, o:o + D_MODEL]
    two =lambda a: jnp.concatenate([a, a])[None, :].astype(F32)
    a_op, bin_op, cout_op, dec = (z[i] for z in p["s5_ops"])
    return {
        "norm_mix": p["norm_mix"][i][None, :].astype(F32),
        "wq": wq[:, perm].astype(BF16), "wk": wk.astype(BF16), "wvt": wv.T.astype(BF16),
        "wu": wu.astype(BF16), "wga": wga.astype(BF16), "wgs": wgs.astype(BF16),
        "qn": two(p["q_norm"][i]), "kn": two(p["k_norm"][i]),
        "bounded": _scores_bounded(p["q_norm"][i], p["k_norm"][i]),
        "s5_a": a_op, "s5_bin": bin_op, "s5_cout": cout_op, "s5_dec": dec,
        "wp": p["w_attn_proj"][i][perm, :].astype(BF16),
        "wa": p["w_glu_a"][i].astype(BF16), "wb": p["w_glu_b"][i].astype(BF16),
        "wo": p["w_out"][i].astype(BF16),
        "norm_mlp": p["norm_mlp"][i][None, :].astype(F32),
        "w1": p["w_ff1"][i].astype(BF16), "w2": p["w_ff2"][i].astype(BF16),
    }


def _tiles(seq_len):
    assert seq_len % (CHUNK * SUBLANES) == 0 and seq_len % LANES == 0
    prefs = dict(proj_rows=512, mlp_rows=512, queries=512, keys=4096)
    tiles = {name: min(seq_len, pref) for name, pref in prefs.items()}
    assert all(seq_len % t == 0 for t in tiles.values()), (seq_len, tiles)
    return tiles


def _trunk(x, layers):
    batch, seq_len, _ = x.shape
    n = batch * seq_len
    rope = _rope_tables(seq_len)
    t = _tiles(seq_len)
    x2 = x.reshape(n, D_MODEL)
    for lw in layers:
        q, k, vt, u, ga, gs = _in_proj(x2, lw, rope, seq_len, t["proj_rows"])
        attn = _attention(lw["bounded"], q, k, vt, batch, seq_len, t["queries"], t["keys"])
        y = _s5(u, lw, batch, seq_len)
        x2 = _mix(x2, attn, y, ga, gs, lw, t["proj_rows"])
        x2 = _mlp(x2, lw, t["mlp_rows"])
    return x2.reshape(batch, seq_len, D_MODEL)


def kernel(x_prompt, x_sample, norm_mix, w_in, q_norm, k_norm, w_attn_proj, ssm_lambda_re, ssm_lambda_im, ssm_log_step, ssm_b_re, ssm_b_im, ssm_c_re, ssm_c_im, ssm_d, w_glu_a, w_glu_b, w_out, norm_mlp, w_ff1, w_ff2):
    params = dict(norm_mix=norm_mix, w_in=w_in, q_norm=q_norm, k_norm=k_norm,
                  w_attn_proj=w_attn_proj, ssm_lambda_re=ssm_lambda_re,
                  ssm_lambda_im=ssm_lambda_im, ssm_log_step=ssm_log_step, ssm_b_re=ssm_b_re,
                  ssm_b_im=ssm_b_im, ssm_c_re=ssm_c_re, ssm_c_im=ssm_c_im, ssm_d=ssm_d,
                  w_glu_a=w_glu_a, w_glu_b=w_glu_b, w_out=w_out, norm_mlp=norm_mlp,
                  w_ff1=w_ff1, w_ff2=w_ff2)
    params["s5_ops"] = jax.vmap(_s5_operators)(
        ssm_lambda_re, ssm_lambda_im, ssm_log_step, ssm_b_re, ssm_b_im, ssm_c_re, ssm_c_im, ssm_d)
    layers =[_layer_weights(i, params) for i in range(norm_mix.shape[0])]
    return (_trunk(x_prompt, layers), _trunk(x_sample, layers))
```
